```python
import math
import jax
import jax.numpy as jnp
from jax import lax
import numpy as np

D_MODEL = 1024
BATCH = 32
SEQ = 256
DEPTH = 2
DEC_BATCH = 8
DEC_SEQ = 4096
PAST_LEN = 512

GRID_W = 64
N_SSD_LAYERS = (DEPTH + 1) // 2
N_FNET_LAYERS = DEPTH // 2
SSD_HEAD_DIM = 64
SSD_HEADS = D_MODEL // SSD_HEAD_DIM
D_SSD = SSD_HEADS * SSD_HEAD_DIM
SSD_STATE = 128
SSD_GROUPS = 4
SSD_CONV = 5
SSD_CHUNK = 128
CONF_CH = D_MODEL
CONF_KERNEL = 31
FNET_GROUPS = 8
D_FF = ((8 * D_MODEL // 3 + 127) // 128) * 128
FFN_CONV = 3
N_MOD = 6
EPS = 1e-6
D_XBC = D_SSD + 2 * SSD_GROUPS * SSD_STATE
D_IN = D_SSD + D_XBC + 2 * SSD_HEADS + 2 * CONF_CH
DT_MIN = 1e-3
DT_MAX = 1e-1

kernel_name = "hybrid_ssd_conformer_fnet_prefix_step"


def rmsnorm(x, g):
    xf = x.astype(jnp.float32)
    y = xf * lax.rsqrt(jnp.mean(xf * xf, axis=-1, keepdims=True) + EPS)
    return (y * g.astype(jnp.float32)).astype(x.dtype)


def layernorm(x, g, b):
    xf = x.astype(jnp.float32)
    mu = jnp.mean(xf, axis=-1, keepdims=True)
    xc = xf - mu
    var = jnp.mean(xc * xc, axis=-1, keepdims=True)
    return (xc * lax.rsqrt(var + EPS) * g.astype(jnp.float32) + b.astype(jnp.float32)).astype(x.dtype)


def dwconv1d(x, w):
    k = w.shape[0]
    return lax.conv_general_dilated(
        x, w.astype(x.dtype)[:, None, :], window_strides=(1,), padding=[(k // 2, k // 2)],
        dimension_numbers=('NWC', 'WIO', 'NWC'), feature_group_count=x.shape[-1])


def dwconv2d(x, w):
    kh, kw = w.shape[0], w.shape[1]
    return lax.conv_general_dilated(
        x, w.astype(x.dtype)[:, :, None, :], window_strides=(1, 1),
        padding=[(kh // 2, kh // 2), (kw // 2, kw // 2)],
        dimension_numbers=('NHWC', 'HWIO', 'NHWC'), feature_group_count=x.shape[-1])


def ssd_scan(x, dt, a, bm, cm, h0):
    b, l, nh, p = x.shape
    g, n = bm.shape[2], bm.shape[3]
    r = nh // g
    q = SSD_CHUNK
    nc = l // q
    xd = (x * dt[..., None]).reshape(b, nc, q, g, r, p)
    la = (dt * a).reshape(b, nc, q, g, r)
    bm = bm.reshape(b, nc, q, g, n)
    cm = cm.reshape(b, nc, q, g, n)
    a_cs = jnp.cumsum(la, axis=2)
    causal = jnp.tril(jnp.ones((q, q), dtype=bool))[None, None, :, :, None, None]
    seg = a_cs[:, :, :, None] - a_cs[:, :, None, :]
    decay = jnp.exp(jnp.where(causal, seg, -jnp.inf))
    cb = jnp.einsum('bcqgn,bcsgn->bcqsg', cm, bm)
    y_diag = jnp.einsum('bcqsg,bcqsgr,bcsgrp->bcqgrp', cb, decay, xd)
    decay_to_end = jnp.exp(a_cs[:, :, -1:] - a_cs)
    chunk_states = jnp.einsum('bcqgn,bcqgr,bcqgrp->bcgrpn', bm, decay_to_end, xd)
    chunk_decay = jnp.exp(a_cs[:, :, -1])

    def step(h, inp):
        s, d = inp
        return h * d[..., None, None] + s, h

    h_last, h_prev = lax.scan(step, h0.reshape(b, g, r, p, n),
                              (jnp.moveaxis(chunk_states, 1, 0), jnp.moveaxis(chunk_decay, 1, 0)))
    y_off = jnp.einsum('bcqgn,cbgrpn,bcqgr->bcqgrp', cm, h_prev, jnp.exp(a_cs))
    y = (y_diag + y_off).reshape(b, l, nh, p)
    return y, h_last.reshape(b, nh, p, n)


def ssd_conformer_mixer(h, h0, w_in, conv_w, conv_b, dt_bias, a_log, d_skip, norm_g,
                        cconv_w, cconv_b, ln_g, ln_b, w_out):
    bsz, l, _ = h.shape
    proj = h @ w_in
    z = proj[..., :D_SSD]
    xbc = proj[..., D_SSD:D_SSD + D_XBC]
    dt_raw = proj[..., D_SSD + D_XBC:D_SSD + D_XBC + 2 * SSD_HEADS]
    glu = proj[..., D_SSD + D_XBC + 2 * SSD_HEADS:]
    xbc = jax.nn.silu(dwconv1d(xbc, conv_w) + conv_b).astype(jnp.float32)
    gn = SSD_GROUPS * SSD_STATE
    xs = xbc[..., :D_SSD].reshape(bsz, l, SSD_HEADS, SSD_HEAD_DIM)
    bm = xbc[..., D_SSD:D_SSD + gn].reshape(bsz, l, SSD_GROUPS, SSD_STATE)
    cm = xbc[..., D_SSD + gn:].reshape(bsz, l, SSD_GROUPS, SSD_STATE)
    dt = jax.nn.softplus(dt_raw.astype(jnp.float32).reshape(bsz, l, 2, SSD_HEADS)
                         + dt_bias.astype(jnp.float32))
    a = -jnp.exp(a_log.astype(jnp.float32))
    h0 = h0.astype(jnp.float32)
    y_f, s_f = ssd_scan(xs, dt[:, :, 0], a[0], bm, cm, h0[:, 0])
    flip = lambda t: jnp.flip(t, axis=1)
    y_b, s_b = ssd_scan(flip(xs), flip(dt[:, :, 1]), a[1], flip(bm), flip(cm), h0[:, 1])
    y = y_f + flip(y_b) + xs * d_skip.astype(jnp.float32)[:, None]
    y = y.reshape(bsz, l, D_SSD) * jax.nn.silu(z.astype(jnp.float32))
    y_ssd = rmsnorm(y, norm_g).astype(h.dtype)
    u = glu[..., :CONF_CH] * jax.nn.sigmoid(glu[..., CONF_CH:])
    u = dwconv1d(u, cconv_w) + cconv_b
    u = jax.nn.silu(layernorm(u, ln_g, ln_b))
    out = jnp.concatenate([y_ssd, u], axis=-1) @ w_out
    return out, jnp.stack([s_f, s_b], axis=1)


def fourier_mixer(h, w, b):
    bsz, l, d = h.shape
    u = h.astype(jnp.float32).reshape(bsz, l, FNET_GROUPS, d // FNET_GROUPS)
    u = jnp.fft.fftn(u, axes=(1, 3), norm='ortho').real
    return u.reshape(bsz, l, d).astype(h.dtype) @ w + b


def conv_ffn(h, w_up, conv_w, w_down, grid_rows):
    bsz, l, _ = h.shape
    u = h @ w_up
    if grid_rows is None:
        u = dwconv1d(u, conv_w[FFN_CONV // 2])
    else:
        u = dwconv2d(u.reshape(bsz, grid_rows, GRID_W, -1), conv_w).reshape(bsz, l, -1)
    val, gate = jnp.split(u, 2, axis=-1)
    return (val * jax.nn.silu(gate)) @ w_down


def trunk(x, cond, h0, grid_rows, p):
    states = []
    for i in range(DEPTH):
        mod = (jax.nn.silu(cond) @ p['w_mod'][i] + p['b_mod'][i]).astype(x.dtype)
        sh1, sc1, gt1, sh2, sc2, gt2 = jnp.split(mod[:, None, :], N_MOD, axis=-1)
        hm = rmsnorm(x, p['g_mix'][i]) * (1 + sc1) + sh1
        j = i // 2
        if i % 2 == 0:
            out, st = ssd_conformer_mixer(
                hm, h0[:, j], p['w_in'][j], p['ssd_conv_w'][j], p['ssd_conv_b'][j],
                p['ssd_dt_bias'][j], p['ssd_a_log'][j], p['ssd_d'][j], p['ssd_norm'][j],
                p['conf_conv_w'][j], p['conf_conv_b'][j], p['conf_ln_g'][j], p['conf_ln_b'][j],
                p['w_out'][j])
            if grid_rows is None:
                states.append(st)
        else:
            out = fourier_mixer(hm, p['fnet_w'][j], p['fnet_b'][j])
        x = x + gt1 * out
        hf = rmsnorm(x, p['g_ffn'][i]) * (1 + sc2) + sh2
        x = x + gt2 * conv_ffn(hf, p['ffn_w_up'][i], p['ffn_conv_w'][i], p['ffn_w_down'][i], grid_rows)
    return rmsnorm(x, p['norm_f']), states


def setup_inputs(seed: int = 0) -> dict:
    key = jax.random.key(seed)
    ks = jax.random.split(key, 32)
    nrm = lambda k, shape, s: jax.random.normal(k, shape, jnp.float32) * s
    dt0 = jnp.exp(jax.random.uniform(ks[12], (N_SSD_LAYERS, 2, SSD_HEADS), jnp.float32)
                  * (math.log(DT_MAX) - math.log(DT_MIN)) + math.log(DT_MIN))
    return {
        'x_prompt': nrm(ks[0], (BATCH, SEQ, D_MODEL), 1.0),
        'x_sample': nrm(ks[1], (DEC_BATCH, DEC_SEQ, D_MODEL), 1.0),
        'c': nrm(ks[2], (DEC_BATCH, D_MODEL), 1.0),
        'state_ssd': nrm(ks[3], (DEC_BATCH, N_SSD_LAYERS, 2, SSD_HEADS, SSD_HEAD_DIM, SSD_STATE), 0.1),
        'c_ctx': nrm(ks[4], (D_MODEL,), 1.0),
        'w_mod': nrm(ks[5], (DEPTH, D_MODEL, N_MOD * D_MODEL), 0.5 * D_MODEL ** -0.5),
        'b_mod': nrm(ks[6], (DEPTH, N_MOD * D_MODEL), 0.01),
        'g_mix': 1.0 + nrm(ks[7], (DEPTH, D_MODEL), 0.02),
        'g_ffn': 1.0 + nrm(ks[8], (DEPTH, D_MODEL), 0.02),
        'w_in': nrm(ks[9], (N_SSD_LAYERS, D_MODEL, D_IN), D_MODEL ** -0.5),
        'ssd_conv_w': nrm(ks[10], (N_SSD_LAYERS, SSD_CONV, D_XBC), SSD_CONV ** -0.5),
        'ssd_conv_b': nrm(ks[11], (N_SSD_LAYERS, D_XBC), 0.01),
        'ssd_dt_bias': dt0 + jnp.log(-jnp.expm1(-dt0)),
        'ssd_a_log': jnp.log(jax.random.uniform(ks[13], (N_SSD_LAYERS, 2, SSD_HEADS), jnp.float32, 1.0, 16.0)),
        'ssd_d': 1.0 + nrm(ks[14], (N_SSD_LAYERS, SSD_HEADS), 0.1),
        'ssd_norm': 1.0 + nrm(ks[15], (N_SSD_LAYERS, D_SSD), 0.02),
        'conf_conv_w': nrm(ks[16], (N_SSD_LAYERS, CONF_KERNEL, CONF_CH), CONF_KERNEL ** -0.5),
        'conf_conv_b': nrm(ks[17], (N_SSD_LAYERS, CONF_CH), 0.01),
        'conf_ln_g': 1.0 + nrm(ks[18], (N_SSD_LAYERS, CONF_CH), 0.02),
        'conf_ln_b': nrm(ks[19], (N_SSD_LAYERS, CONF_CH), 0.01),
        'w_out': nrm(ks[20], (N_SSD_LAYERS, D_SSD + CONF_CH, D_MODEL), (D_SSD + CONF_CH) ** -0.5),
        'fnet_w': nrm(ks[21], (N_FNET_LAYERS, D_MODEL, D_MODEL), D_MODEL ** -0.5),
        'fnet_b': nrm(ks[22], (N_FNET_LAYERS, D_MODEL), 0.01),
        'ffn_w_up': nrm(ks[23], (DEPTH, D_MODEL, 2 * D_FF), D_MODEL ** -0.5),
        'ffn_conv_w': nrm(ks[24], (DEPTH, FFN_CONV, FFN_CONV, 2 * D_FF), 1.0 / FFN_CONV),
        'ffn_w_down': nrm(ks[25], (DEPTH, D_FF, D_MODEL), D_FF ** -0.5),
        'norm_f': 1.0 + nrm(ks[26], (D_MODEL,), 0.02),
    }


def reference(x_prompt, x_sample, c, state_ssd, c_ctx, w_mod, b_mod, g_mix, g_ffn, w_in,
              ssd_conv_w, ssd_conv_b, ssd_dt_bias, ssd_a_log, ssd_d, ssd_norm,
              conf_conv_w, conf_conv_b, conf_ln_g, conf_ln_b, w_out, fnet_w, fnet_b,
              ffn_w_up, ffn_conv_w, ffn_w_down, norm_f):
    p = dict(w_mod=w_mod, b_mod=b_mod, g_mix=g_mix, g_ffn=g_ffn, w_in=w_in,
             ssd_conv_w=ssd_conv_w, ssd_conv_b=ssd_conv_b, ssd_dt_bias=ssd_dt_bias,
             ssd_a_log=ssd_a_log, ssd_d=ssd_d, ssd_norm=ssd_norm,
             conf_conv_w=conf_conv_w, conf_conv_b=conf_conv_b, conf_ln_g=conf_ln_g,
             conf_ln_b=conf_ln_b, w_out=w_out, fnet_w=fnet_w, fnet_b=fnet_b,
             ffn_w_up=ffn_w_up, ffn_conv_w=ffn_conv_w, ffn_w_down=ffn_w_down, norm_f=norm_f)
    h0_ctx = jnp.zeros((x_prompt.shape[0], N_SSD_LAYERS, 2, SSD_HEADS, SSD_HEAD_DIM, SSD_STATE),
                       jnp.float32)
    y_prompt, ctx_states = trunk(x_prompt, c_ctx[None, :], h0_ctx, None, p)
    new_state_ssd = jnp.stack(ctx_states, axis=1).astype(x_prompt.dtype)
    grid_rows = x_sample.shape[1] // GRID_W
    y_sample, _ = trunk(x_sample, c, state_ssd, grid_rows, p)
    return (y_prompt, y_sample, new_state_ssd)
```

```python
import functools
import math

import numpy as np
import jax
import jax.numpy as jnp
from jax import lax
from jax.experimental import pallas as pl
from jax.experimental.pallas import tpu as pltpu

F32 = jnp.float32
BF16 = jnp.bfloat16

D_MODEL = 1024
GRID_W = 64
SSD_HEAD_DIM = 64
SSD_HEADS = 16
D_SSD = SSD_HEADS * SSD_HEAD_DIM
SSD_STATE = 128
SSD_GROUPS = 4
SSD_CONV = 5
SSD_CHUNK = 128
CONF_CH = D_MODEL
CONF_KERNEL = 31
FNET_GROUPS = 8
D_FF = 2816
N_MOD = 6
EPS = 1e-6
D_XBC = D_SSD + 2 * SSD_GROUPS * SSD_STATE

LANE = 128
SUBLANE = 8
VMEM_LIMIT = 56 * 1024 * 1024

MOD_ROWS = 16
CTX_ROW = 8


def _dot(a, b):
    return jnp.dot(a, b, preferred_element_type=F32)


def _silu(v):
    return v * jax.nn.sigmoid(v)


def _split2(v):
    hi = v.astype(BF16)
    lo = (v - hi.astype(F32)).astype(BF16)
    return hi, lo


def _split3(v):
    hi = v.astype(BF16)
    r = v - hi.astype(F32)
    mid = r.astype(BF16)
    lo = (r - mid.astype(F32)).astype(BF16)
    return hi, mid, lo


def _rms(x):
    return x * lax.rsqrt(jnp.mean(x * x, axis=-1, keepdims=True) + EPS)


def _modnorm(x, g, scale, shift):
    return (_rms(x) * g) * (1.0 + scale) + shift


def _cparams(sem):
    return pltpu.CompilerParams(dimension_semantics=sem, vmem_limit_bytes=VMEM_LIMIT)


def _const_spec(shape):
    nd = len(shape)
    return pl.BlockSpec(shape, lambda *_: (0,) * nd)


def _mod_kernel(cond_ref, w_ref, b_ref, o_ref):
    s = _silu(cond_ref[...])
    s_hi, s_lo = _split2(s)
    w_hi, w_lo = _split2(w_ref[0])
    acc = _dot(s_hi, w_hi) + _dot(s_hi, w_lo) + _dot(s_lo, w_hi)
    o_ref[0] = acc + b_ref[0]


def _modulation(cond, w_mod, b_mod):
    depth = w_mod.shape[0]
    n = w_mod.shape[2]
    tn = D_MODEL
    return pl.pallas_call(
        _mod_kernel,
        grid=(depth, n // tn),
        in_specs=[
            _const_spec((MOD_ROWS, D_MODEL)),
            pl.BlockSpec((1, D_MODEL, tn), lambda i, j: (i, 0, j)),
            pl.BlockSpec((1, 1, tn), lambda i, j: (i, 0, j)),
        ],
        out_specs=pl.BlockSpec((1, MOD_ROWS, tn), lambda i, j: (i, 0, j)),
        out_shape=jax.ShapeDtypeStruct((depth, MOD_ROWS, n), F32),
        compiler_params=_cparams(("parallel", "parallel")),
        name="modulation",
    )(cond, w_mod, b_mod.reshape(depth, 1, n))


def _mod_spec(seq_tiles, per_batch):
    if per_batch:
        return pl.BlockSpec((1, N_MOD, D_MODEL), lambda i: (i // seq_tiles, 0, 0))
    return pl.BlockSpec((1, N_MOD, D_MODEL), lambda i: (CTX_ROW, 0, 0))


def _inproj_kernel(x_ref, mod_ref, g_ref, wz_ref, wx_ref, wg_ref, wd_ref,
                   z_ref, xbc_ref, glu_ref, dt_ref):
    m = mod_ref[0]
    h = _modnorm(x_ref[...], g_ref[...], m[1:2], m[0:1]).astype(BF16)
    z_ref[...] = _dot(h, wz_ref[...])
    xbc_ref[...] = _dot(h, wx_ref[...])
    glu_ref[...] = _dot(h, wg_ref[...])
    dt_ref[...] = _dot(h, wd_ref[...])


def _in_proj(x2, mod, g, wz, wx, wg, wd, seq, per_batch, tm):
    t = x2.shape[0]
    row = lambda w: pl.BlockSpec((tm, w), lambda i: (i, 0))
    return pl.pallas_call(
        _inproj_kernel,
        grid=(t // tm,),
        in_specs=[row(D_MODEL), _mod_spec(seq // tm, per_batch), _const_spec((1, D_MODEL)),
                  _const_spec(wz.shape), _const_spec(wx.shape), _const_spec(wg.shape),
                  _const_spec(wd.shape)],
        out_specs=[row(D_SSD), row(D_XBC), row(2 * CONF_CH), row(LANE)],
        out_shape=[jax.ShapeDtypeStruct((t, D_SSD), F32), jax.ShapeDtypeStruct((t, D_XBC), F32),
                   jax.ShapeDtypeStruct((t, 2 * CONF_CH), F32), jax.ShapeDtypeStruct((t, LANE), F32)],
        compiler_params=_cparams(("parallel",)),
        name="in_proj",
    )(x2, mod, g, wz, wx, wg, wd)


_HALO = SUBLANE
_CONV_PAD = SSD_CONV // 2


def _ssd_conv_silu(main_ref, prev_ref, next_ref, has_prev, has_next, cw_ref, cb_ref, ext_ref):
    q = SSD_CHUNK
    ext_ref[0:_HALO, :] = jnp.where(has_prev, prev_ref[...], 0.0)
    ext_ref[_HALO:_HALO + q, :] = main_ref[...]
    ext_ref[_HALO + q:2 * _HALO + q, :] = jnp.where(has_next, next_ref[...], 0.0)
    acc = jnp.broadcast_to(cb_ref[...], (q, D_XBC))
    for j in range(SSD_CONV):
        off = _HALO - _CONV_PAD + j
        acc = acc + cw_ref[j:j + 1, :] * ext_ref[off:off + q, :]
    return _silu(acc)


def _ssd_direction(xbc, dt_raw, dtb, a_row, ht_ref, d, rev):
    q = SSD_CHUNK
    xs = xbc[:, :D_SSD]
    bm = xbc[:, D_SSD:D_SSD + SSD_GROUPS * SSD_STATE]
    cm = xbc[:, D_SSD + SSD_GROUPS * SSD_STATE:]
    dt = jax.nn.softplus(dt_raw + dtb)
    la = dt * a_row

    ri = lax.broadcasted_iota(jnp.int32, (q, q), 0)
    ci = lax.broadcasted_iota(jnp.int32, (q, q), 1)
    keep = (ci >= ri) if rev else (ci <= ri)
    tri = jnp.where(keep, 1.0, 0.0).astype(BF16)
    tri_t = jnp.where((ri >= ci) if rev else (ri <= ci), 1.0, 0.0).astype(BF16)
    cum = sum(_dot(tri, p) for p in _split3(la))
    cum_t = sum(_dot(p, tri_t) for p in _split3(la.T))

    end = 0 if rev else q - 1
    cum_end = cum[end:end + 1, :]
    e_off = jnp.exp(cum)
    e_dte = jnp.exp(cum_end - cum)
    e_cd = jnp.exp(cum_end)

    er = lax.broadcasted_iota(jnp.int32, (LANE, D_SSD), 0)
    ec = lax.broadcasted_iota(jnp.int32, (LANE, D_SSD), 1)
    expand = jnp.where(er == ec // SSD_HEAD_DIM + d * SSD_HEADS, 1.0, 0.0).astype(BF16)

    def _expand(v):
        hi, lo = _split2(v)
        return _dot(hi, expand) + _dot(lo, expand)

    dt_x = _expand(dt)
    eoff_x = _expand(e_off)
    edte_x = _expand(e_dte)
    cd_x = _expand(jnp.broadcast_to(e_cd, (SUBLANE, LANE)))[0:1, :]

    xd = xs * dt_x
    lane = lax.broadcasted_iota(jnp.int32, (q, D_SSD), 1)
    even_head = (lane // SSD_HEAD_DIM) % 2 == 0
    xd_even = jnp.where(even_head, xd, 0.0).astype(BF16)
    xd_odd = jnp.where(even_head, 0.0, xd).astype(BF16)
    xdte = (xd * edte_x).astype(BF16)
    cb16 = cm.astype(BF16)

    gw = SSD_HEADS // SSD_GROUPS * SSD_HEAD_DIM
    ys = []
    for g in range(SSD_GROUPS):
        cg = cb16[:, g * SSD_STATE:(g + 1) * SSD_STATE]
        bg_t = bm[:, g * SSD_STATE:(g + 1) * SSD_STATE].T.astype(BF16)
        cbm = _dot(cg, bg_t)
        ht_g = ht_ref[d, :, g * gw:(g + 1) * gw]
        y_g = _dot(cg, ht_g.astype(BF16)) * eoff_x[:, g * gw:(g + 1) * gw]
        halves = []
        for pr in range(2):
            acc = None
            for r in range(2):
                h = g * 4 + pr * 2 + r
                c = d * SSD_HEADS + h
                seg = cum[:, c:c + 1] - cum_t[c:c + 1, :]
                decay = jnp.exp(jnp.where(keep, seg, -jnp.inf))
                m = (cbm * decay).astype(BF16)
                src = xd_even if r == 0 else xd_odd
                blk = (h // 2) * LANE
                part = _dot(m, src[:, blk:blk + LANE])
                acc = part if acc is None else acc + part
            halves.append(acc)
        ys.append(y_g + jnp.concatenate(halves, axis=1))
        s_t = _dot(bg_t, xdte[:, g * gw:(g + 1) * gw])
        ht_ref[d, :, g * gw:(g + 1) * gw] = ht_g * cd_x[:, g * gw:(g + 1) * gw] + s_t
    return jnp.concatenate(ys, axis=1), xs


def _ssd_kernel(xf_ref, xfp_ref, xfn_ref, xb_ref, xbp_ref, xbn_ref, dtf_ref, dtb_ref, h0_ref,
                cw_ref, cb_ref, dtbias_ref, alog_ref, dskip_ref,
                yf_ref, yb_ref, *rest, nc, emit_state):
    if emit_state:
        st_ref, ht_ref, ext_ref = rest
    else:
        ht_ref, ext_ref = rest
    k = pl.program_id(1)

    @pl.when(k == 0)
    def _():
        for d in range(2):
            ht_ref[d] = h0_ref[0, d].T

    a_row = -jnp.exp(alog_ref[...])
    dtbias = dtbias_ref[...]

    xbc = _ssd_conv_silu(xf_ref, xfp_ref, xfn_ref, k > 0, k < nc - 1, cw_ref, cb_ref, ext_ref)
    y, xs = _ssd_direction(xbc, dtf_ref[...], dtbias, a_row, ht_ref, 0, False)
    yf_ref[...] = y + xs * dskip_ref[...]

    xbc = _ssd_conv_silu(xb_ref, xbp_ref, xbn_ref, k < nc - 1, k > 0, cw_ref, cb_ref, ext_ref)
    y, _ = _ssd_direction(xbc, dtb_ref[...], dtbias, a_row, ht_ref, 1, True)
    yb_ref[...] = y

    if emit_state:
        @pl.when(k == nc - 1)
        def _():
            for d in range(2):
                st_ref[0, d] = ht_ref[d].T


def _ssd_scan(xbc, dt, h0t, cw, cb, dtbias, alog, dskip, nb, seq, emit_state):
    t = xbc.shape[0]
    q = SSD_CHUNK
    nc = seq // q
    hb = q // _HALO
    last_hb = t // _HALO - 1

    def fwd(b, k):
        return b * nc + k

    def bwd(b, k):
        return b * nc + nc - 1 - k

    def specs(ch):
        return [pl.BlockSpec((q, D_XBC), lambda b, k: (ch(b, k), 0)),
                pl.BlockSpec((_HALO, D_XBC), lambda b, k: (jnp.maximum(ch(b, k) * hb - 1, 0), 0)),
                pl.BlockSpec((_HALO, D_XBC), lambda b, k: (jnp.minimum((ch(b, k) + 1) * hb, last_hb), 0))]

    in_specs = (specs(fwd) + specs(bwd) + [
        pl.BlockSpec((q, LANE), lambda b, k: (fwd(b, k), 0)),
        pl.BlockSpec((q, LANE), lambda b, k: (bwd(b, k), 0)),
        pl.BlockSpec((1, 2, D_SSD, SSD_STATE), lambda b, k: (b, 0, 0, 0)),
        _const_spec(cw.shape), _const_spec(cb.shape), _const_spec(dtbias.shape),
        _const_spec(alog.shape), _const_spec(dskip.shape)])
    out_specs = [pl.BlockSpec((q, D_SSD), lambda b, k: (fwd(b, k), 0)),
                 pl.BlockSpec((q, D_SSD), lambda b, k: (bwd(b, k), 0))]
    out_shape = [jax.ShapeDtypeStruct((t, D_SSD), F32), jax.ShapeDtypeStruct((t, D_SSD), F32)]
    if emit_state:
        out_specs.append(pl.BlockSpec((1, 2, D_SSD, SSD_STATE), lambda b, k: (b, 0, 0, 0)))
        out_shape.append(jax.ShapeDtypeStruct((nb, 2, D_SSD, SSD_STATE), F32))
    return pl.pallas_call(
        functools.partial(_ssd_kernel, nc=nc, emit_state=emit_state),
        grid=(nb, nc),
        in_specs=in_specs, out_specs=out_specs, out_shape=out_shape,
        scratch_shapes=[pltpu.VMEM((2, SSD_STATE, D_SSD), F32),
                        pltpu.VMEM((q + 2 * _HALO, D_XBC), F32)],
        compiler_params=_cparams(("parallel", "arbitrary")),
        name="ssd_scan",
    )(xbc, xbc, xbc, xbc, xbc, xbc, dt, dt, h0t, cw, cb, dtbias, alog, dskip)


_CHALO = 16
_CPAD = CONF_KERNEL // 2
_ROWBLK = SUBLANE


def _mixout_kernel(x_ref, yf_ref, yb_ref, z_ref, glu_ref, glup_ref, glun_ref, mod_ref,
                   nrm_ref, cw_ref, cb_ref, lng_ref, lnb_ref, wo_ref, gffn_ref,
                   x1_ref, hf_ref, ext_ref, sh_ref, conv_ref, *, tm, seq_tiles):
    i = pl.program_id(0)
    has_prev = i % seq_tiles > 0
    has_next = i % seq_tiles < seq_tiles - 1

    def glu(ref):
        v = ref[...]
        return v[:, :CONF_CH] * jax.nn.sigmoid(v[:, CONF_CH:])

    n_ext = tm + 2 * _CHALO
    ext_ref[0:_CHALO, :] = jnp.where(has_prev, glu(glup_ref), 0.0)
    ext_ref[_CHALO:_CHALO + tm, :] = glu(glu_ref)
    ext_ref[_CHALO + tm:n_ext, :] = jnp.where(has_next, glu(glun_ref), 0.0)
    n_sh = n_ext - SUBLANE
    for r in range(SUBLANE):
        sh_ref[r, 0:n_sh, :] = ext_ref[r:r + n_sh, :]

    def row_block(rb, carry):
        base = pl.multiple_of(rb * _ROWBLK, _ROWBLK)
        acc = jnp.broadcast_to(cb_ref[...], (_ROWBLK, CONF_CH))
        for j in range(CONF_KERNEL):
            off = _CHALO - _CPAD + j
            qq, r = divmod(off, SUBLANE)
            acc = acc + cw_ref[j:j + 1, :] * sh_ref[r, pl.ds(base + qq * SUBLANE, _ROWBLK), :]
        conv_ref[pl.ds(base, _ROWBLK), :] = acc
        return carry

    lax.fori_loop(0, tm // _ROWBLK, row_block, 0)

    u = conv_ref[...]
    mu = jnp.mean(u, axis=-1, keepdims=True)
    uc = u - mu
    var = jnp.mean(uc * uc, axis=-1, keepdims=True)
    u = _silu(uc * lax.rsqrt(var + EPS) * lng_ref[...] + lnb_ref[...])

    y = (yf_ref[...] + yb_ref[...]) * _silu(z_ref[...])
    y = _rms(y) * nrm_ref[...]

    out = _dot(y.astype(BF16), wo_ref[0:D_SSD, :]) + _dot(u.astype(BF16), wo_ref[D_SSD:, :])
    m = mod_ref[0]
    x1 = x_ref[...] + m[2:3] * out
    x1_ref[...] = x1
    hf_ref[...] = _modnorm(x1, gffn_ref[...], m[4:5], m[3:4]).astype(BF16)


def _mix_out(x2, yf, yb, z, glu, mod, nrm, cw, cb, lng, lnb, wo, gffn, seq, per_batch, tm):
    t = x2.shape[0]
    seq_tiles = seq // tm
    hb = tm // _CHALO
    last_hb = t // _CHALO - 1
    row = lambda w: pl.BlockSpec((tm, w), lambda i: (i, 0))
    vec = _const_spec((1, D_MODEL))
    return pl.pallas_call(
        functools.partial(_mixout_kernel, tm=tm, seq_tiles=seq_tiles),
        grid=(t // tm,),
        in_specs=[row(D_MODEL), row(D_SSD), row(D_SSD), row(D_SSD), row(2 * CONF_CH),
                  pl.BlockSpec((_CHALO, 2 * CONF_CH), lambda i: (jnp.maximum(i * hb - 1, 0), 0)),
                  pl.BlockSpec((_CHALO, 2 * CONF_CH), lambda i: (jnp.minimum((i + 1) * hb, last_hb), 0)),
                  _mod_spec(seq_tiles, per_batch),
                  vec, _const_spec(cw.shape), vec, vec, vec, _const_spec(wo.shape), vec],
        out_specs=[row(D_MODEL), row(D_MODEL)],
        out_shape=[jax.ShapeDtypeStruct((t, D_MODEL), F32), jax.ShapeDtypeStruct((t, D_MODEL), BF16)],
        scratch_shapes=[pltpu.VMEM((tm + 2 * _CHALO, CONF_CH), F32),
                        pltpu.VMEM((SUBLANE, tm + 2 * _CHALO - SUBLANE, CONF_CH), F32),
                        pltpu.VMEM((tm, CONF_CH), F32)],
        compiler_params=_cparams(("parallel",)),
        name="mix_out",
    )(x2, yf, yb, z, glu, glu, glu, mod, nrm, cw, cb, lng, lnb, wo, gffn)


_FF_CHUNK = 256
_FHALO = 128


def _ffn_kernel(*refs, tm, seq_tiles, period, row_taps, halo, final_norm):
    if halo:
        (hf_ref, hfp_ref, hfn_ref, x1_ref, mod_ref, wup_ref, cw_ref, wdn_ref, nf_ref,
         o_ref, ext_ref, ul_ref, uc_ref, ur_ref, acc_ref) = refs
    else:
        (hf_ref, x1_ref, mod_ref, wup_ref, cw_ref, wdn_ref, nf_ref,
         o_ref, ext_ref, ul_ref, uc_ref, ur_ref, acc_ref) = refs
    i = pl.program_id(0)
    pad = SUBLANE
    n_ext = tm + 2 * halo

    if halo:
        has_prev = i % seq_tiles > 0
        has_next = i % seq_tiles < seq_tiles - 1
        ext_ref[0:halo, :] = jnp.where(has_prev, hfp_ref[_FHALO - halo:_FHALO, :], 0.0).astype(BF16)
        ext_ref[halo:halo + tm, :] = hf_ref[...]
        ext_ref[halo + tm:n_ext, :] = jnp.where(has_next, hfn_ref[0:halo, :], 0.0).astype(BF16)
        hext = ext_ref[...]
    else:
        hext = hf_ref[...]

    col = (lax.broadcasted_iota(jnp.int32, (n_ext, 1), 0) + (period - halo % period)) % period
    first_col = col == 0
    last_col = col == period - 1

    zeros_pad = jnp.zeros((pad, _FF_CHUNK), F32)
    for ref in (ul_ref, uc_ref, ur_ref):
        for half in range(2):
            ref[half, 0:pad, :] = zeros_pad
            ref[half, pad + n_ext:2 * pad + n_ext, :] = zeros_pad

    acc_ref[...] = jnp.zeros((tm, D_MODEL), F32)
    for c in range(D_FF // _FF_CHUNK):
        outs = []
        for half in range(2):
            c0 = half * D_FF + c * _FF_CHUNK
            u = _dot(hext, wup_ref[:, c0:c0 + _FF_CHUNK])
            ul_ref[half, pad:pad + n_ext, :] = jnp.where(last_col, 0.0, u)
            uc_ref[half, pad:pad + n_ext, :] = u
            ur_ref[half, pad:pad + n_ext, :] = jnp.where(first_col, 0.0, u)
            conv = None
            for dr in row_taps:
                base = pad + halo + dr * period
                w0 = (dr + 1) * 3
                for dc, ref in ((-1, ul_ref), (0, uc_ref), (1, ur_ref)):
                    term = cw_ref[w0 + dc + 1:w0 + dc + 2, c0:c0 + _FF_CHUNK] * \
                        ref[half, base + dc:base + dc + tm, :]
                    conv = term if conv is None else conv + term
            outs.append(conv)
        g = (outs[0] * _silu(outs[1])).astype(BF16)
        acc_ref[...] += _dot(g, wdn_ref[c * _FF_CHUNK:(c + 1) * _FF_CHUNK, :])

    m = mod_ref[0]
    x2 = x1_ref[...] + m[5:6] * acc_ref[...]
    if final_norm:
        x2 = _rms(x2) * nf_ref[...]
    o_ref[...] = x2


def _conv_ffn(hf, x1, mod, wup, cw, wdn, nf, seq, per_batch, tm, latent, final_norm):
    t = hf.shape[0]
    seq_tiles = seq // tm
    if latent:
        period, row_taps, halo = GRID_W, (-1, 0, 1), GRID_W + 16
    else:
        period, row_taps, halo = seq, (0,), 0
    n_ext = tm + 2 * halo
    row = lambda w: pl.BlockSpec((tm, w), lambda i: (i, 0))
    in_specs = [row(D_MODEL)]
    args = [hf]
    if halo:
        hb = tm // _FHALO
        last_hb = t // _FHALO - 1
        in_specs += [pl.BlockSpec((_FHALO, D_MODEL), lambda i: (jnp.maximum(i * hb - 1, 0), 0)),
                     pl.BlockSpec((_FHALO, D_MODEL), lambda i: (jnp.minimum((i + 1) * hb, last_hb), 0))]
        args += [hf, hf]
    in_specs += [row(D_MODEL), _mod_spec(seq_tiles, per_batch), _const_spec(wup.shape),
                 _const_spec(cw.shape), _const_spec(wdn.shape), _const_spec((1, D_MODEL))]
    args += [x1, mod, wup, cw, wdn, nf]
    ubuf = pltpu.VMEM((2, n_ext + 2 * SUBLANE, _FF_CHUNK), F32)
    return pl.pallas_call(
        functools.partial(_ffn_kernel, tm=tm, seq_tiles=seq_tiles, period=period,
                          row_taps=row_taps, halo=halo, final_norm=final_norm),
        grid=(t // tm,),
        in_specs=in_specs,
        out_specs=row(D_MODEL),
        out_shape=jax.ShapeDtypeStruct((t, D_MODEL), F32),
        scratch_shapes=[pltpu.VMEM((n_ext, D_MODEL), BF16), ubuf, ubuf, ubuf,
                        pltpu.VMEM((tm, D_MODEL), F32)],
        compiler_params=_cparams(("parallel",)),
        name="conv_ffn",
    )(*args)


def _prenorm_kernel(x_ref, mod_ref, g_ref, o_ref):
    m = mod_ref[0]
    o_ref[...] = _modnorm(x_ref[...], g_ref[...], m[1:2], m[0:1])


def _prenorm(x2, mod, g, seq, per_batch, tm):
    t = x2.shape[0]
    row = pl.BlockSpec((tm, D_MODEL), lambda i: (i, 0))
    return pl.pallas_call(
        _prenorm_kernel, grid=(t // tm,),
        in_specs=[row, _mod_spec(seq // tm, per_batch), _const_spec((1, D_MODEL))],
        out_specs=row, out_shape=jax.ShapeDtypeStruct((t, D_MODEL), F32),
        compiler_params=_cparams(("parallel",)), name="prenorm",
    )(x2, mod, g)


def _dft_tables(n):
    k = np.arange(n, dtype=np.int64)
    ang = 2.0 * np.pi * ((k[:, None] * k[None, :]) % n) / n
    s = 1.0 / math.sqrt(n)
    return np.cos(ang) * s, np.sin(ang) * s


def _seqdft_direct_kernel(h_ref, m_ref, pr_ref, pi_ref, *, seq):
    p = _dot(m_ref[...], h_ref[...].astype(BF16))
    pr_ref[...] = p[0:seq].astype(BF16)
    pi_ref[...] = p[seq:2 * seq].astype(BF16)


def _seq_dft_direct(h2, nb, seq):
    c, s = _dft_tables(seq)
    mat = jnp.asarray(np.concatenate([c, -s], axis=0), BF16)
    t = h2.shape[0]
    tc = 512
    blk = pl.BlockSpec((seq, tc), lambda b, j: (b, j))
    return pl.pallas_call(
        functools.partial(_seqdft_direct_kernel, seq=seq),
        grid=(nb, D_MODEL // tc),
        in_specs=[blk, _const_spec(mat.shape)],
        out_specs=[blk, blk],
        out_shape=[jax.ShapeDtypeStruct((t, D_MODEL), BF16)] * 2,
        compiler_params=_cparams(("parallel", "parallel")), name="seq_dft_direct",
    )(h2, mat)


def _seqdft_2stage_kernel(h_ref, m1_ref, m2_ref, pr_ref, pi_ref, yr_ref, yi_ref, *, w):
    for n2 in range(w):
        xs = h_ref[pl.ds(n2, w, stride=w), :].astype(BF16)
        y = _dot(m1_ref[n2], xs)
        yr_ref[n2 * w:(n2 + 1) * w, :] = y[0:w]
        yi_ref[n2 * w:(n2 + 1) * w, :] = y[w:2 * w]
    m2 = m2_ref[...]
    for k1 in range(w):
        yk = jnp.concatenate([yr_ref[pl.ds(k1, w, stride=w), :],
                              yi_ref[pl.ds(k1, w, stride=w), :]], axis=0).astype(BF16)
        p = _dot(m2, yk)
        pr_ref[pl.ds(k1, w, stride=w), :] = p[0:w]
        pi_ref[pl.ds(k1, w, stride=w), :] = p[w:2 * w]


def _seq_dft_2stage(h2, nb, seq):
    w = GRID_W
    assert seq == w * w
    k1 = np.arange(w, dtype=np.int64)[:, None]
    n1 = np.arange(w, dtype=np.int64)[None, :]
    m1 = np.zeros((w, 2 * w, w), np.float64)
    for n2 in range(w):
        ang = 2.0 * np.pi * ((k1 * (w * n1 + n2)) % seq) / seq
        m1[n2, :w] = np.cos(ang)
        m1[n2, w:] = -np.sin(ang)
    m1 /= math.sqrt(seq)
    ang2 = 2.0 * np.pi * ((k1 * n1) % w) / w
    c2, s2 = np.cos(ang2), np.sin(ang2)
    m2 = np.block([[c2, s2], [-s2, c2]])
    t = h2.shape[0]
    tc = LANE
    blk = pl.BlockSpec((seq, tc), lambda b, j: (b, j))
    return pl.pallas_call(
        functools.partial(_seqdft_2stage_kernel, w=w),
        grid=(nb, D_MODEL // tc),
        in_specs=[blk, _const_spec(m1.shape), _const_spec(m2.shape)],
        out_specs=[blk, blk],
        out_shape=[jax.ShapeDtypeStruct((t, D_MODEL), F32)] * 2,
        scratch_shapes=[pltpu.VMEM((seq, tc), F32), pltpu.VMEM((seq, tc), F32)],
        compiler_params=_cparams(("parallel", "parallel")), name="seq_dft_2stage",
    )(h2, jnp.asarray(m1, BF16), jnp.asarray(m2, BF16))


def _fnetw_kernel(dft_ref, w_ref, o_ref):
    o_ref[...] = _dot(dft_ref[...], w_ref[...].astype(BF16)).astype(BF16)


def _fnet_weights(fnet_w):
    gsz = D_MODEL // FNET_GROUPS
    c, s = _dft_tables(gsz)
    eye = np.eye(FNET_GROUPS)
    dft = jnp.asarray(np.concatenate([np.kron(eye, c), np.kron(eye, s)], axis=0), BF16)
    tmw = 512
    return pl.pallas_call(
        _fnetw_kernel, grid=(2 * D_MODEL // tmw,),
        in_specs=[pl.BlockSpec((tmw, D_MODEL), lambda i: (i, 0)), _const_spec((D_MODEL, D_MODEL))],
        out_specs=pl.BlockSpec((tmw, D_MODEL), lambda i: (i, 0)),
        out_shape=jax.ShapeDtypeStruct((2 * D_MODEL, D_MODEL), BF16),
        compiler_params=_cparams(("parallel",)), name="fnet_weights",
    )(dft, fnet_w)


def _fnetout_kernel(pr_ref, pi_ref, x_ref, mod_ref, w_ref, b_ref, gffn_ref, x1_ref, hf_ref):
    out = (_dot(pr_ref[...].astype(BF16), w_ref[0:D_MODEL, :])
           + _dot(pi_ref[...].astype(BF16), w_ref[D_MODEL:, :]) + b_ref[...])
    m = mod_ref[0]
    x1 = x_ref[...] + m[2:3] * out
    x1_ref[...] = x1
    hf_ref[...] = _modnorm(x1, gffn_ref[...], m[4:5], m[3:4]).astype(BF16)


def _fnet_out(pr, pi, x2, mod, wcs, b, gffn, seq, per_batch, tm):
    t = x2.shape[0]
    row = pl.BlockSpec((tm, D_MODEL), lambda i: (i, 0))
    vec = _const_spec((1, D_MODEL))
    return pl.pallas_call(
        _fnetout_kernel, grid=(t // tm,),
        in_specs=[row, row, row, _mod_spec(seq // tm, per_batch), _const_spec(wcs.shape), vec, vec],
        out_specs=[row, row],
        out_shape=[jax.ShapeDtypeStruct((t, D_MODEL), F32), jax.ShapeDtypeStruct((t, D_MODEL), BF16)],
        compiler_params=_cparams(("parallel",)), name="fnet_out",
    )(pr, pi, x2, mod, wcs, b, gffn)


def _trunk(x, mods, h0t, per_batch, latent, p):
    nb, seq, _ = x.shape
    t = nb * seq
    x2 = x.reshape(t, D_MODEL)
    tm = 256

    z, xbc, glu, dt = _in_proj(x2, mods[0], p['g_mix'][0], p['wz'], p['wx'], p['wg'], p['wd'],
                               seq, per_batch, tm)
    scan = _ssd_scan(xbc, dt, h0t, p['ssd_cw'], p['ssd_cb'], p['dtbias'], p['alog'], p['dskip'],
                     nb, seq, emit_state=not latent)
    yf, yb = scan[0], scan[1]
    state = scan[2] if not latent else None
    x1, hf = _mix_out(x2, yf, yb, z, glu, mods[0], p['ssd_norm'], p['conf_cw'], p['conf_cb'],
                      p['ln_g'], p['ln_b'], p['w_out'], p['g_ffn'][0], seq, per_batch, tm)
    ftm = 512
    x2 = _conv_ffn(hf, x1, mods[0], p['w_up'][0], p['ffn_cw'][0], p['w_down'][0], p['norm_f'],
                   seq, per_batch, ftm, latent, final_norm=False)

    hm = _prenorm(x2, mods[1], p['g_mix'][1], seq, per_batch, tm)
    if seq == GRID_W * GRID_W:
        pr, pi = _seq_dft_2stage(hm, nb, seq)
    else:
        pr, pi = _seq_dft_direct(hm, nb, seq)
    x1, hf = _fnet_out(pr, pi, x2, mods[1], p['fnet_wcs'], p['fnet_b'], p['g_ffn'][1],
                       seq, per_batch, tm)
    y = _conv_ffn(hf, x1, mods[1], p['w_up'][1], p['ffn_cw'][1], p['w_down'][1], p['norm_f'],
                  seq, per_batch, ftm, latent, final_norm=True)
    return y.reshape(nb, seq, D_MODEL), state


def kernel(x_prompt, x_sample, c, state_ssd, c_ctx, w_mod, b_mod, g_mix, g_ffn, w_in, ssd_conv_w, ssd_conv_b, ssd_dt_bias, ssd_a_log, ssd_d, ssd_norm, conf_conv_w, conf_conv_b, conf_ln_g, conf_ln_b, w_out, fnet_w, fnet_b, ffn_w_up, ffn_conv_w, ffn_w_down, norm_f):
    depth = w_mod.shape[0]
    nb_ctx = x_prompt.shape[0]
    nb_lat = x_sample.shape[0]
    assert nb_lat <= CTX_ROW

    cond = jnp.zeros((MOD_ROWS, D_MODEL), F32).at[:nb_lat].set(c).at[CTX_ROW].set(c_ctx)
    mod = _modulation(cond, w_mod, b_mod).reshape(depth, MOD_ROWS, N_MOD, D_MODEL)
    mods = [mod[i] for i in range(depth)]

    def pad_lanes(v, fill=0.0):
        return jnp.pad(v, ((0, 0), (0, LANE - v.shape[1])), constant_values=fill)

    w_in0 = w_in[0].astype(BF16)
    o_dt = D_SSD + D_XBC
    o_glu = o_dt + 2 * SSD_HEADS
    vec = lambda v: v.reshape(1, -1)
    p = dict(
        g_mix=[vec(g_mix[i]) for i in range(depth)],
        g_ffn=[vec(g_ffn[i]) for i in range(depth)],
        wz=w_in0[:, :D_SSD], wx=w_in0[:, D_SSD:o_dt], wd=pad_lanes(w_in0[:, o_dt:o_glu]), wg=w_in0[:, o_glu:],
        ssd_cw=jnp.pad(ssd_conv_w[0], ((0, SUBLANE - SSD_CONV), (0, 0))),
        ssd_cb=vec(ssd_conv_b[0]),
        dtbias=pad_lanes(ssd_dt_bias[0].reshape(1, -1)),
        alog=pad_lanes(ssd_a_log[0].reshape(1, -1)),
        dskip=vec(jnp.repeat(ssd_d[0], SSD_HEAD_DIM)),
        ssd_norm=vec(ssd_norm[0]),
        conf_cw=jnp.pad(conf_conv_w[0], ((0, 1), (0, 0))),
        conf_cb=vec(conf_conv_b[0]), ln_g=vec(conf_ln_g[0]), ln_b=vec(conf_ln_b[0]),
        w_out=w_out[0].astype(BF16),
        fnet_wcs=_fnet_weights(fnet_w[0]), fnet_b=vec(fnet_b[0]),
        w_up=[ffn_w_up[i].astype(BF16) for i in range(depth)],
        ffn_cw=[jnp.pad(ffn_conv_w[i].reshape(9, -1), ((0, 7), (0, 0))) for i in range(depth)],
        w_down=[ffn_w_down[i].astype(BF16) for i in range(depth)],
        norm_f=vec(norm_f),
    )

    h0_ctx = jnp.zeros((nb_ctx, 2, D_SSD, SSD_STATE), F32)
    y_prompt, st = _trunk(x_prompt, mods, h0_ctx, False, False, p)
    new_state = st.reshape(nb_ctx, 1, 2, SSD_HEADS, SSD_HEAD_DIM, SSD_STATE).astype(x_prompt.dtype)

    h0_lat = state_ssd[:, 0].astype(F32).reshape(nb_lat, 2, D_SSD, SSD_STATE)
    y_sample, _ = _trunk(x_sample, mods, h0_lat, True, True, p)
    return (y_prompt, y_sample, new_state)
```

```python
import functools
import math

import numpy as np
import jax
import jax.numpy as jnp
from jax import lax
from jax.experimental import pallas as pl
from jax.experimental.pallas import tpu as pltpu

F32 = jnp.float32
BF16 = jnp.bfloat16

D_MODEL = 1024
GRID_W = 64
SSD_HEAD_DIM = 64
SSD_HEADS = 16
D_SSD = SSD_HEADS * SSD_HEAD_DIM
SSD_STATE = 128
SSD_GROUPS = 4
SSD_CONV = 5
SSD_CHUNK = 128
CONF_CH = D_MODEL
CONF_KERNEL = 31
FNET_GROUPS = 8
D_FF = 2816
N_MOD = 6
EPS = 1e-6
D_XBC = D_SSD + 2 * SSD_GROUPS * SSD_STATE

LANE = 128
SUBLANE = 8
VMEM_LIMIT = 56 * 1024 * 1024

MOD_ROWS = 16
CTX_ROW = 8


def _dot(a, b):
    return jnp.dot(a, b, preferred_element_type=F32)


def _silu(v):
    return v * jax.nn.sigmoid(v)


def _split2(v):
    hi = v.astype(BF16)
    lo = (v - hi.astype(F32)).astype(BF16)
    return hi, lo


def _split3(v):
    hi = v.astype(BF16)
    r = v - hi.astype(F32)
    mid = r.astype(BF16)
    lo = (r - mid.astype(F32)).astype(BF16)
    return hi, mid, lo


def _rms(x):
    return x * lax.rsqrt(jnp.mean(x * x, axis=-1, keepdims=True) + EPS)


def _modnorm(x, g, scale, shift):
    return (_rms(x) * g) * (1.0 + scale) + shift


def _cparams(sem, flags=None):
    return pltpu.CompilerParams(dimension_semantics=sem, vmem_limit_bytes=VMEM_LIMIT, flags=flags)


def _const_spec(shape, single=False):
    nd = len(shape)
    if single:
        return pl.BlockSpec(shape, lambda *_: (0,) * nd, pipeline_mode=pl.Buffered(1))
    return pl.BlockSpec(shape, lambda *_: (0,) * nd)


def _mod_kernel(cond_ref, w_ref, b_ref, o_ref):
    s = _silu(cond_ref[...])
    s_hi, s_lo = _split2(s)
    w_hi, w_lo = _split2(w_ref[0])
    acc = _dot(s_hi, w_hi) + _dot(s_hi, w_lo) + _dot(s_lo, w_hi)
    o_ref[0] = acc + b_ref[0]


def _modulation(cond, w_mod, b_mod):
    depth = w_mod.shape[0]
    n = w_mod.shape[2]
    tn = D_MODEL
    return pl.pallas_call(
        _mod_kernel,
        grid=(depth, n // tn),
        in_specs=[
            _const_spec((MOD_ROWS, D_MODEL)),
            pl.BlockSpec((1, D_MODEL, tn), lambda i, j: (i, 0, j)),
            pl.BlockSpec((1, 1, tn), lambda i, j: (i, 0, j)),
        ],
        out_specs=pl.BlockSpec((1, MOD_ROWS, tn), lambda i, j: (i, 0, j)),
        out_shape=jax.ShapeDtypeStruct((depth, MOD_ROWS, n), F32),
        compiler_params=_cparams(("parallel", "parallel")),
        name="modulation",
    )(cond, w_mod, b_mod.reshape(depth, 1, n))


def _mod_spec(seq_tiles, per_batch):
    if per_batch:
        return pl.BlockSpec((1, N_MOD, D_MODEL), lambda i: (i // seq_tiles, 0, 0))
    return pl.BlockSpec((1, N_MOD, D_MODEL), lambda i: (CTX_ROW, 0, 0))


def _inproj_kernel(x_ref, mod_ref, g_ref, wz_ref, wx_ref, wg_ref, wd_ref,
                   z_ref, xbc_ref, glu_ref, dt_ref):
    m = mod_ref[0]
    h = _modnorm(x_ref[...], g_ref[...], m[1:2], m[0:1]).astype(BF16)
    z_ref[...] = _dot(h, wz_ref[...])
    xbc_ref[...] = _dot(h, wx_ref[...])
    glu_ref[...] = _dot(h, wg_ref[...])
    dt_ref[...] = _dot(h, wd_ref[...])


def _in_proj(x2, mod, g, wz, wx, wg, wd, seq, per_batch, tm):
    t = x2.shape[0]
    row = lambda w: pl.BlockSpec((tm, w), lambda i: (i, 0))
    return pl.pallas_call(
        _inproj_kernel,
        grid=(t // tm,),
        in_specs=[row(D_MODEL), _mod_spec(seq // tm, per_batch), _const_spec((1, D_MODEL)),
                  _const_spec(wz.shape), _const_spec(wx.shape), _const_spec(wg.shape),
                  _const_spec(wd.shape)],
        out_specs=[row(D_SSD), row(D_XBC), row(2 * CONF_CH), row(LANE)],
        out_shape=[jax.ShapeDtypeStruct((t, D_SSD), F32), jax.ShapeDtypeStruct((t, D_XBC), F32),
                   jax.ShapeDtypeStruct((t, 2 * CONF_CH), F32), jax.ShapeDtypeStruct((t, LANE), F32)],
        compiler_params=_cparams(("parallel",)),
        name="in_proj",
    )(x2, mod, g, wz, wx, wg, wd)


_HALO = SUBLANE
_CONV_PAD = SSD_CONV // 2


def _ssd_conv_silu(main_ref, prev_ref, next_ref, has_prev, has_next, cw_ref, cb_ref, ext_ref):
    q = SSD_CHUNK
    ext_ref[0:_HALO, :] = jnp.where(has_prev, prev_ref[...], 0.0)
    ext_ref[_HALO:_HALO + q, :] = main_ref[...]
    ext_ref[_HALO + q:2 * _HALO + q, :] = jnp.where(has_next, next_ref[...], 0.0)
    acc = jnp.broadcast_to(cb_ref[...], (q, D_XBC))
    for j in range(SSD_CONV):
        off = _HALO - _CONV_PAD + j
        acc = acc + cw_ref[j:j + 1, :] * ext_ref[off:off + q, :]
    return _silu(acc)


def _ssd_direction(xbc, dt_raw, dtb, a_row, ht_ref, d, rev):
    q = SSD_CHUNK
    xs = xbc[:, :D_SSD]
    bm = xbc[:, D_SSD:D_SSD + SSD_GROUPS * SSD_STATE]
    cm = xbc[:, D_SSD + SSD_GROUPS * SSD_STATE:]
    dt = jax.nn.softplus(dt_raw + dtb)
    la = dt * a_row

    ri = lax.broadcasted_iota(jnp.int32, (q, q), 0)
    ci = lax.broadcasted_iota(jnp.int32, (q, q), 1)
    keep = (ci >= ri) if rev else (ci <= ri)
    tri = jnp.where(keep, 1.0, 0.0).astype(BF16)
    tri_t = jnp.where((ri >= ci) if rev else (ri <= ci), 1.0, 0.0).astype(BF16)
    cum = sum(_dot(tri, p) for p in _split3(la))
    cum_t = sum(_dot(p, tri_t) for p in _split3(la.T))

    end = 0 if rev else q - 1
    cum_end = cum[end:end + 1, :]
    e_off = jnp.exp(cum)
    e_dte = jnp.exp(cum_end - cum)
    e_cd = jnp.exp(cum_end)

    er = lax.broadcasted_iota(jnp.int32, (LANE, D_SSD), 0)
    ec = lax.broadcasted_iota(jnp.int32, (LANE, D_SSD), 1)
    expand = jnp.where(er == ec // SSD_HEAD_DIM + d * SSD_HEADS, 1.0, 0.0).astype(BF16)

    def _expand(v):
        hi, lo = _split2(v)
        return _dot(hi, expand) + _dot(lo, expand)

    dt_x = _expand(dt)
    eoff_x = _expand(e_off)
    edte_x = _expand(e_dte)
    cd_x = _expand(jnp.broadcast_to(e_cd, (SUBLANE, LANE)))[0:1, :]

    xd = xs * dt_x
    lane = lax.broadcasted_iota(jnp.int32, (q, D_SSD), 1)
    even_head = (lane // SSD_HEAD_DIM) % 2 == 0
    xd_even = jnp.where(even_head, xd, 0.0).astype(BF16)
    xd_odd = jnp.where(even_head, 0.0, xd).astype(BF16)
    xdte = (xd * edte_x).astype(BF16)
    cb16 = cm.astype(BF16)

    gw = SSD_HEADS // SSD_GROUPS * SSD_HEAD_DIM
    ys = []
    for g in range(SSD_GROUPS):
        cg = cb16[:, g * SSD_STATE:(g + 1) * SSD_STATE]
        bg_t = bm[:, g * SSD_STATE:(g + 1) * SSD_STATE].T.astype(BF16)
        cbm = _dot(cg, bg_t)
        ht_g = ht_ref[d, :, g * gw:(g + 1) * gw]
        y_g = _dot(cg, ht_g.astype(BF16)) * eoff_x[:, g * gw:(g + 1) * gw]
        halves = []
        for pr in range(2):
            acc = None
            for r in range(2):
                h = g * 4 + pr * 2 + r
                c = d * SSD_HEADS + h
                seg = cum[:, c:c + 1] - cum_t[c:c + 1, :]
                decay = jnp.exp(jnp.where(keep, seg, -jnp.inf))
                m = (cbm * decay).astype(BF16)
                src = xd_even if r == 0 else xd_odd
                blk = (h // 2) * LANE
                part = _dot(m, src[:, blk:blk + LANE])
                acc = part if acc is None else acc + part
            halves.append(acc)
        ys.append(y_g + jnp.concatenate(halves, axis=1))
        s_t = _dot(bg_t, xdte[:, g * gw:(g + 1) * gw])
        ht_ref[d, :, g * gw:(g + 1) * gw] = ht_g * cd_x[:, g * gw:(g + 1) * gw] + s_t
    return jnp.concatenate(ys, axis=1), xs


def _ssd_kernel(xf_ref, xfp_ref, xfn_ref, xb_ref, xbp_ref, xbn_ref, dtf_ref, dtb_ref, h0_ref,
                cw_ref, cb_ref, dtbias_ref, alog_ref, dskip_ref,
                yf_ref, yb_ref, *rest, nc, emit_state):
    if emit_state:
        st_ref, ht_ref, ext_ref = rest
    else:
        ht_ref, ext_ref = rest
    k = pl.program_id(1)

    @pl.when(k == 0)
    def _():
        for d in range(2):
            ht_ref[d] = h0_ref[0, d].T

    a_row = -jnp.exp(alog_ref[...])
    dtbias = dtbias_ref[...]

    xbc = _ssd_conv_silu(xf_ref, xfp_ref, xfn_ref, k > 0, k < nc - 1, cw_ref, cb_ref, ext_ref)
    y, xs = _ssd_direction(xbc, dtf_ref[...], dtbias, a_row, ht_ref, 0, False)
    yf_ref[...] = y + xs * dskip_ref[...]

    xbc = _ssd_conv_silu(xb_ref, xbp_ref, xbn_ref, k < nc - 1, k > 0, cw_ref, cb_ref, ext_ref)
    y, _ = _ssd_direction(xbc, dtb_ref[...], dtbias, a_row, ht_ref, 1, True)
    yb_ref[...] = y

    if emit_state:
        @pl.when(k == nc - 1)
        def _():
            for d in range(2):
                st_ref[0, d] = ht_ref[d].T


def _ssd_scan(xbc, dt, h0t, cw, cb, dtbias, alog, dskip, nb, seq, emit_state):
    t = xbc.shape[0]
    q = SSD_CHUNK
    nc = seq // q
    hb = q // _HALO
    last_hb = t // _HALO - 1

    def fwd(b, k):
        return b * nc + k

    def bwd(b, k):
        return b * nc + nc - 1 - k

    def specs(ch):
        return [pl.BlockSpec((q, D_XBC), lambda b, k: (ch(b, k), 0)),
                pl.BlockSpec((_HALO, D_XBC), lambda b, k: (jnp.maximum(ch(b, k) * hb - 1, 0), 0)),
                pl.BlockSpec((_HALO, D_XBC), lambda b, k: (jnp.minimum((ch(b, k) + 1) * hb, last_hb), 0))]

    in_specs = (specs(fwd) + specs(bwd) + [
        pl.BlockSpec((q, LANE), lambda b, k: (fwd(b, k), 0)),
        pl.BlockSpec((q, LANE), lambda b, k: (bwd(b, k), 0)),
        pl.BlockSpec((1, 2, D_SSD, SSD_STATE), lambda b, k: (b, 0, 0, 0)),
        _const_spec(cw.shape), _const_spec(cb.shape), _const_spec(dtbias.shape),
        _const_spec(alog.shape), _const_spec(dskip.shape)])
    out_specs = [pl.BlockSpec((q, D_SSD), lambda b, k: (fwd(b, k), 0)),
                 pl.BlockSpec((q, D_SSD), lambda b, k: (bwd(b, k), 0))]
    out_shape = [jax.ShapeDtypeStruct((t, D_SSD), F32), jax.ShapeDtypeStruct((t, D_SSD), F32)]
    if emit_state:
        out_specs.append(pl.BlockSpec((1, 2, D_SSD, SSD_STATE), lambda b, k: (b, 0, 0, 0)))
        out_shape.append(jax.ShapeDtypeStruct((nb, 2, D_SSD, SSD_STATE), F32))
    return pl.pallas_call(
        functools.partial(_ssd_kernel, nc=nc, emit_state=emit_state),
        grid=(nb, nc),
        in_specs=in_specs, out_specs=out_specs, out_shape=out_shape,
        scratch_shapes=[pltpu.VMEM((2, SSD_STATE, D_SSD), F32),
                        pltpu.VMEM((q + 2 * _HALO, D_XBC), F32)],
        compiler_params=_cparams(("parallel", "arbitrary")),
        name="ssd_scan",
    )(xbc, xbc, xbc, xbc, xbc, xbc, dt, dt, h0t, cw, cb, dtbias, alog, dskip)


_CHALO = 16
_CPAD = CONF_KERNEL // 2
_ROWBLK = 2 * SUBLANE


def _mixout_kernel(x_ref, yf_ref, yb_ref, z_ref, glu_ref, glup_ref, glun_ref, mod_ref,
                   nrm_ref, cw_ref, cb_ref, lng_ref, lnb_ref, wo_ref, gffn_ref,
                   x1_ref, hf_ref, ext_ref, sh_ref, conv_ref, *, tm, seq_tiles):
    i = pl.program_id(0)
    has_prev = i % seq_tiles > 0
    has_next = i % seq_tiles < seq_tiles - 1

    def glu(ref):
        v = ref[...]
        return v[:, :CONF_CH] * jax.nn.sigmoid(v[:, CONF_CH:])

    n_ext = tm + 2 * _CHALO
    ext_ref[0:_CHALO, :] = jnp.where(has_prev, glu(glup_ref), 0.0)
    ext_ref[_CHALO:_CHALO + tm, :] = glu(glu_ref)
    ext_ref[_CHALO + tm:n_ext, :] = jnp.where(has_next, glu(glun_ref), 0.0)
    e = ext_ref[...]
    for r in range(SUBLANE):
        sh_ref[r, SUBLANE - r:SUBLANE - r + n_ext, :] = e

    def row_block(rb, carry):
        base = pl.multiple_of(rb * _ROWBLK, _ROWBLK)
        nsub = _ROWBLK // SUBLANE
        accs = [jnp.broadcast_to(cb_ref[...], (SUBLANE, CONF_CH))] * nsub
        for j in range(CONF_KERNEL):
            off = _CHALO - _CPAD + j
            qq, r = divmod(off, SUBLANE)
            w8 = cw_ref[j * SUBLANE:(j + 1) * SUBLANE, :]
            accs = [a + w8 * sh_ref[r, pl.ds(base + (qq + 1 + k) * SUBLANE, SUBLANE), :]
                    for k, a in enumerate(accs)]
        for k, a in enumerate(accs):
            conv_ref[pl.ds(base + k * SUBLANE, SUBLANE), :] = a
        return carry

    lax.fori_loop(0, tm // _ROWBLK, row_block, 0)

    u = conv_ref[...]
    mu = jnp.mean(u, axis=-1, keepdims=True)
    uc = u - mu
    var = jnp.mean(uc * uc, axis=-1, keepdims=True)
    u = _silu(uc * lax.rsqrt(var + EPS) * lng_ref[...] + lnb_ref[...])

    y = (yf_ref[...] + yb_ref[...]) * _silu(z_ref[...])
    y = _rms(y) * nrm_ref[...]

    out = _dot(y.astype(BF16), wo_ref[0:D_SSD, :]) + _dot(u.astype(BF16), wo_ref[D_SSD:, :])
    m = mod_ref[0]
    x1 = x_ref[...] + m[2:3] * out
    x1_ref[...] = x1
    hf_ref[...] = _modnorm(x1, gffn_ref[...], m[4:5], m[3:4]).astype(BF16)


def _mix_out(x2, yf, yb, z, glu, mod, nrm, cw, cb, lng, lnb, wo, gffn, seq, per_batch, tm):
    t = x2.shape[0]
    seq_tiles = seq // tm
    hb = tm // _CHALO
    last_hb = t // _CHALO - 1
    row = lambda w: pl.BlockSpec((tm, w), lambda i: (i, 0))
    vec = _const_spec((1, D_MODEL))
    return pl.pallas_call(
        functools.partial(_mixout_kernel, tm=tm, seq_tiles=seq_tiles),
        grid=(t // tm,),
        in_specs=[row(D_MODEL), row(D_SSD), row(D_SSD), row(D_SSD), row(2 * CONF_CH),
                  pl.BlockSpec((_CHALO, 2 * CONF_CH), lambda i: (jnp.maximum(i * hb - 1, 0), 0)),
                  pl.BlockSpec((_CHALO, 2 * CONF_CH), lambda i: (jnp.minimum((i + 1) * hb, last_hb), 0)),
                  _mod_spec(seq_tiles, per_batch),
                  vec, _const_spec(cw.shape), vec, vec, vec, _const_spec(wo.shape), vec],
        out_specs=[row(D_MODEL), row(D_MODEL)],
        out_shape=[jax.ShapeDtypeStruct((t, D_MODEL), F32), jax.ShapeDtypeStruct((t, D_MODEL), BF16)],
        scratch_shapes=[pltpu.VMEM((tm + 2 * _CHALO, CONF_CH), F32),
                        pltpu.VMEM((SUBLANE, tm + 2 * _CHALO + SUBLANE, CONF_CH), F32),
                        pltpu.VMEM((tm, CONF_CH), F32)],
        compiler_params=_cparams(("parallel",)),
        name="mix_out",
    )(x2, yf, yb, z, glu, glu, glu, mod, nrm, cw, cb, lng, lnb, wo, gffn)


_FF_CHUNK = 256
_FF_NCH = D_FF // _FF_CHUNK
_FHALO = 128
_FF_ROWBLK = 32
_FF_PAD = SUBLANE
_FFN_FLAGS = None


def _ffn_kernel(*refs, tm, seq_tiles, period, row_taps, halo, final_norm):
    refs = list(refs)
    hf_ref = refs.pop(0)
    if halo:
        hfp_ref, hfn_ref = refs.pop(0), refs.pop(0)
    x1_ref, mod_ref, wup_ref, cw_ref, wdn_ref = [refs.pop(0) for _ in range(5)]
    if final_norm:
        nf_ref, o_ref = refs.pop(0), refs.pop(0)
    else:
        modn_ref, gn_ref, o_ref, hn_ref = [refs.pop(0) for _ in range(4)]
    ext_ref = refs.pop(0) if halo else hf_ref
    sl_ref, sc_ref, sr_ref, ml_ref, mr_ref, g_ref, acc_ref = refs
    i = pl.program_id(0)
    pad = _FF_PAD
    n_ext = tm + 2 * halo

    if halo:
        has_prev = i % seq_tiles > 0
        has_next = i % seq_tiles < seq_tiles - 1
        ext_ref[0:halo, :] = jnp.where(has_prev, hfp_ref[_FHALO - halo:_FHALO, :], 0.0).astype(BF16)
        ext_ref[halo:halo + tm, :] = hf_ref[...]
        ext_ref[halo + tm:n_ext, :] = jnp.where(has_next, hfn_ref[0:halo, :], 0.0).astype(BF16)

    col = (lax.broadcasted_iota(jnp.int32, (n_ext, _FF_CHUNK), 0) + (period - halo % period)) % period
    ml_ref[...] = jnp.where(col == period - 1, 0.0, 1.0)
    mr_ref[...] = jnp.where(col == 0, 0.0, 1.0)
    zeros_pad = jnp.zeros((2 * pad, _FF_CHUNK), F32)
    for ref in (sl_ref, sr_ref):
        for k in range(4):
            ref[k, 0:2 * pad, :] = zeros_pad
            ref[k, n_ext:n_ext + 2 * pad, :] = zeros_pad
    acc_ref[...] = jnp.zeros((tm, D_MODEL), F32)

    def up(c, slot):
        for half in range(2):
            u = _dot(ext_ref[...], wup_ref[half * _FF_NCH + c])
            k = slot * 2 + half
            sc_ref[k, pad:pad + n_ext, :] = u
            sl_ref[k, pad + 1:pad + 1 + n_ext, :] = u * ml_ref[...]
            sr_ref[k, pad - 1:pad - 1 + n_ext, :] = u * mr_ref[...]

    def conv(c, slot):
        w = [cw_ref[half * _FF_NCH + c] for half in range(2)]
        for rb in range(tm // _FF_ROWBLK):
            r0 = rb * _FF_ROWBLK
            outs = []
            for half in range(2):
                k = slot * 2 + half
                acc = None
                for dr in row_taps:
                    base = pad + halo + dr * period + r0
                    for dc, ref in enumerate((sl_ref, sc_ref, sr_ref)):
                        tap = (dr + 1) * 3 + dc
                        term = w[half][tap:tap + 1, :] * ref[k, pl.ds(base, _FF_ROWBLK), :]
                        acc = term if acc is None else acc + term
                outs.append(acc)
            g_ref[slot, r0:r0 + _FF_ROWBLK, :] = (outs[0] * _silu(outs[1])).astype(BF16)

    def down(c, slot):
        acc_ref[...] += _dot(g_ref[slot], wdn_ref[c])

    last = _FF_NCH - 1
    up(0, 0)
    up(1, 1)
    conv(0, 0)

    for c in range(1, last + 1):
        if c < last:
            up(c + 1, (c + 1) % 2)
        conv(c, c % 2)
        down(c - 1, (c - 1) % 2)
    down(last, last % 2)

    m = mod_ref[0]
    x2 = x1_ref[...] + m[5:6] * acc_ref[...]
    if final_norm:
        o_ref[...] = _rms(x2) * nf_ref[...]
    else:
        o_ref[...] = x2
        mn = modn_ref[0]
        hn_ref[...] = _modnorm(x2, gn_ref[...], mn[1:2], mn[0:1])


def _conv_ffn(hf, x1, mod, wup, cw, wdn, tail, seq, per_batch, tm, latent, final_norm):
    t = hf.shape[0]
    seq_tiles = seq // tm
    if latent:
        period, row_taps, halo = GRID_W, (-1, 0, 1), GRID_W + 16
    else:
        period, row_taps, halo = seq, (0,), 0
    n_ext = tm + 2 * halo
    row = lambda w: pl.BlockSpec((tm, w), lambda i: (i, 0))
    in_specs = [row(D_MODEL)]
    args = [hf]
    scratch = []
    if halo:
        hb = tm // _FHALO
        last_hb = t // _FHALO - 1
        in_specs += [pl.BlockSpec((_FHALO, D_MODEL), lambda i: (jnp.maximum(i * hb - 1, 0), 0)),
                     pl.BlockSpec((_FHALO, D_MODEL), lambda i: (jnp.minimum((i + 1) * hb, last_hb), 0))]
        args += [hf, hf]
        scratch.append(pltpu.VMEM((n_ext, D_MODEL), BF16))
    in_specs += [row(D_MODEL), _mod_spec(seq_tiles, per_batch), _const_spec(wup.shape, single=True),
                 _const_spec(cw.shape), _const_spec(wdn.shape, single=True)]
    args += [x1, mod, wup, cw, wdn]
    vec = _const_spec((1, D_MODEL))
    if final_norm:
        in_specs += [vec]
        out_specs = row(D_MODEL)
        out_shape = jax.ShapeDtypeStruct((t, D_MODEL), F32)
    else:
        in_specs += [_mod_spec(seq_tiles, per_batch), vec]
        out_specs = [row(D_MODEL), row(D_MODEL)]
        out_shape = [jax.ShapeDtypeStruct((t, D_MODEL), F32)] * 2
    args += list(tail)
    ubuf = pltpu.VMEM((4, n_ext + 2 * _FF_PAD, _FF_CHUNK), F32)
    mask = pltpu.VMEM((n_ext, _FF_CHUNK), F32)
    scratch += [ubuf, ubuf, ubuf, mask, mask,
                pltpu.VMEM((2, tm, _FF_CHUNK), BF16), pltpu.VMEM((tm, D_MODEL), F32)]
    return pl.pallas_call(
        functools.partial(_ffn_kernel, tm=tm, seq_tiles=seq_tiles, period=period,
                          row_taps=row_taps, halo=halo, final_norm=final_norm),
        grid=(t // tm,),
        in_specs=in_specs,
        out_specs=out_specs,
        out_shape=out_shape,
        scratch_shapes=scratch,
        compiler_params=_cparams(("parallel",), _FFN_FLAGS),
        name="conv_ffn",
    )(*args)


def _dft_tables(n):
    k = np.arange(n, dtype=np.int64)
    ang = 2.0 * np.pi * ((k[:, None] * k[None, :]) % n) / n
    s = 1.0 / math.sqrt(n)
    return np.cos(ang) * s, np.sin(ang) * s


def _seqdft_direct_kernel(h_ref, m_ref, pr_ref, pi_ref, *, seq):
    p = _dot(m_ref[...], h_ref[...].astype(BF16))
    pr_ref[...] = p[0:seq].astype(BF16)
    pi_ref[...] = p[seq:2 * seq].astype(BF16)


def _seq_dft_direct(h2, nb, seq):
    c, s = _dft_tables(seq)
    mat = jnp.asarray(np.concatenate([c, -s], axis=0), BF16)
    t = h2.shape[0]
    tc = 512
    blk = pl.BlockSpec((seq, tc), lambda b, j: (b, j))
    return pl.pallas_call(
        functools.partial(_seqdft_direct_kernel, seq=seq),
        grid=(nb, D_MODEL // tc),
        in_specs=[blk, _const_spec(mat.shape)],
        out_specs=[blk, blk],
        out_shape=[jax.ShapeDtypeStruct((t, D_MODEL), BF16)] * 2,
        compiler_params=_cparams(("parallel", "parallel")), name="seq_dft_direct",
    )(h2, mat)


def _seqdft_2stage_kernel(h_ref, m1_ref, m2_ref, pr_ref, pi_ref, yr_ref, yi_ref, *, w):
    for n2 in range(w):
        xs = h_ref[pl.ds(n2, w, stride=w), :].astype(BF16)
        y = _dot(m1_ref[n2], xs)
        yr_ref[n2 * w:(n2 + 1) * w, :] = y[0:w]
        yi_ref[n2 * w:(n2 + 1) * w, :] = y[w:2 * w]
    m2 = m2_ref[...]
    for k1 in range(w):
        yk = jnp.concatenate([yr_ref[pl.ds(k1, w, stride=w), :],
                              yi_ref[pl.ds(k1, w, stride=w), :]], axis=0).astype(BF16)
        p = _dot(m2, yk)
        pr_ref[pl.ds(k1, w, stride=w), :] = p[0:w]
        pi_ref[pl.ds(k1, w, stride=w), :] = p[w:2 * w]


def _seq_dft_2stage(h2, nb, seq):
    w = GRID_W
    assert seq == w * w
    k1 = np.arange(w, dtype=np.int64)[:, None]
    n1 = np.arange(w, dtype=np.int64)[None, :]
    m1 = np.zeros((w, 2 * w, w), np.float64)
    for n2 in range(w):
        ang = 2.0 * np.pi * ((k1 * (w * n1 + n2)) % seq) / seq
        m1[n2, :w] = np.cos(ang)
        m1[n2, w:] = -np.sin(ang)
    m1 /= math.sqrt(seq)
    ang2 = 2.0 * np.pi * ((k1 * n1) % w) / w
    c2, s2 = np.cos(ang2), np.sin(ang2)
    m2 = np.block([[c2, s2], [-s2, c2]])
    t = h2.shape[0]
    tc = LANE
    blk = pl.BlockSpec((seq, tc), lambda b, j: (b, j))
    return pl.pallas_call(
        functools.partial(_seqdft_2stage_kernel, w=w),
        grid=(nb, D_MODEL // tc),
        in_specs=[blk, _const_spec(m1.shape), _const_spec(m2.shape)],
        out_specs=[blk, blk],
        out_shape=[jax.ShapeDtypeStruct((t, D_MODEL), F32)] * 2,
        scratch_shapes=[pltpu.VMEM((seq, tc), F32), pltpu.VMEM((seq, tc), F32)],
        compiler_params=_cparams(("parallel", "parallel")), name="seq_dft_2stage",
    )(h2, jnp.asarray(m1, BF16), jnp.asarray(m2, BF16))


def _fnetw_kernel(dft_ref, w_ref, o_ref):
    o_ref[...] = _dot(dft_ref[...], w_ref[...].astype(BF16)).astype(BF16)


def _fnet_weights(fnet_w):
    gsz = D_MODEL // FNET_GROUPS
    c, s = _dft_tables(gsz)
    eye = np.eye(FNET_GROUPS)
    dft = jnp.asarray(np.concatenate([np.kron(eye, c), np.kron(eye, s)], axis=0), BF16)
    tmw = 512
    return pl.pallas_call(
        _fnetw_kernel, grid=(2 * D_MODEL // tmw,),
        in_specs=[pl.BlockSpec((tmw, D_MODEL), lambda i: (i, 0)), _const_spec((D_MODEL, D_MODEL))],
        out_specs=pl.BlockSpec((tmw, D_MODEL), lambda i: (i, 0)),
        out_shape=jax.ShapeDtypeStruct((2 * D_MODEL, D_MODEL), BF16),
        compiler_params=_cparams(("parallel",)), name="fnet_weights",
    )(dft, fnet_w)


def _fnetout_kernel(pr_ref, pi_ref, x_ref, mod_ref, w_ref, b_ref, gffn_ref, x1_ref, hf_ref):
    out = (_dot(pr_ref[...].astype(BF16), w_ref[0:D_MODEL, :])
           + _dot(pi_ref[...].astype(BF16), w_ref[D_MODEL:, :]) + b_ref[...])
    m = mod_ref[0]
    x1 = x_ref[...] + m[2:3] * out
    x1_ref[...] = x1
    hf_ref[...] = _modnorm(x1, gffn_ref[...], m[4:5], m[3:4]).astype(BF16)


def _fnet_out(pr, pi, x2, mod, wcs, b, gffn, seq, per_batch, tm):
    t = x2.shape[0]
    row = pl.BlockSpec((tm, D_MODEL), lambda i: (i, 0))
    vec = _const_spec((1, D_MODEL))
    return pl.pallas_call(
        _fnetout_kernel, grid=(t // tm,),
        in_specs=[row, row, row, _mod_spec(seq // tm, per_batch), _const_spec(wcs.shape), vec, vec],
        out_specs=[row, row],
        out_shape=[jax.ShapeDtypeStruct((t, D_MODEL), F32), jax.ShapeDtypeStruct((t, D_MODEL), BF16)],
        compiler_params=_cparams(("parallel",)), name="fnet_out",
    )(pr, pi, x2, mod, wcs, b, gffn)


def _trunk(x, mods, h0t, per_batch, latent, p):
    nb, seq, _ = x.shape
    t = nb * seq
    x2 = x.reshape(t, D_MODEL)
    tm = 256

    z, xbc, glu, dt = _in_proj(x2, mods[0], p['g_mix'][0], p['wz'], p['wx'], p['wg'], p['wd'],
                               seq, per_batch, tm)
    scan = _ssd_scan(xbc, dt, h0t, p['ssd_cw'], p['ssd_cb'], p['dtbias'], p['alog'], p['dskip'],
                     nb, seq, emit_state=not latent)
    yf, yb = scan[0], scan[1]
    state = scan[2] if not latent else None
    x1, hf = _mix_out(x2, yf, yb, z, glu, mods[0], p['ssd_norm'], p['conf_cw'], p['conf_cb'],
                      p['ln_g'], p['ln_b'], p['w_out'], p['g_ffn'][0], seq, per_batch, tm)
    ftm = 512
    x2, hm = _conv_ffn(hf, x1, mods[0], p['w_up'][0], p['ffn_cw'][0], p['w_down'][0],
                       (mods[1], p['g_mix'][1]), seq, per_batch, ftm, latent, final_norm=False)

    if seq == GRID_W * GRID_W:
        pr, pi = _seq_dft_2stage(hm, nb, seq)
    else:
        pr, pi = _seq_dft_direct(hm, nb, seq)
    x1, hf = _fnet_out(pr, pi, x2, mods[1], p['fnet_wcs'], p['fnet_b'], p['g_ffn'][1],
                       seq, per_batch, tm)
    y = _conv_ffn(hf, x1, mods[1], p['w_up'][1], p['ffn_cw'][1], p['w_down'][1], (p['norm_f'],),
                  seq, per_batch, ftm, latent, final_norm=True)
    return y.reshape(nb, seq, D_MODEL), state


def kernel(x_prompt, x_sample, c, state_ssd, c_ctx, w_mod, b_mod, g_mix, g_ffn, w_in, ssd_conv_w, ssd_conv_b, ssd_dt_bias, ssd_a_log, ssd_d, ssd_norm, conf_conv_w, conf_conv_b, conf_ln_g, conf_ln_b, w_out, fnet_w, fnet_b, ffn_w_up, ffn_conv_w, ffn_w_down, norm_f):
    depth = w_mod.shape[0]
    nb_ctx = x_prompt.shape[0]
    nb_lat = x_sample.shape[0]
    assert nb_lat <= CTX_ROW

    cond = jnp.zeros((MOD_ROWS, D_MODEL), F32).at[:nb_lat].set(c).at[CTX_ROW].set(c_ctx)
    mod = _modulation(cond, w_mod, b_mod).reshape(depth, MOD_ROWS, N_MOD, D_MODEL)
    mods = [mod[i] for i in range(depth)]

    def pad_lanes(v, fill=0.0):
        return jnp.pad(v, ((0, 0), (0, LANE - v.shape[1])), constant_values=fill)

    w_in0 = w_in[0].astype(BF16)
    o_dt = D_SSD + D_XBC
    o_glu = o_dt + 2 * SSD_HEADS
    vec = lambda v: v.reshape(1, -1)
    p = dict(
        g_mix=[vec(g_mix[i]) for i in range(depth)],
        g_ffn=[vec(g_ffn[i]) for i in range(depth)],
        wz=w_in0[:, :D_SSD], wx=w_in0[:, D_SSD:o_dt], wd=pad_lanes(w_in0[:, o_dt:o_glu]), wg=w_in0[:, o_glu:],
        ssd_cw=jnp.pad(ssd_conv_w[0], ((0, SUBLANE - SSD_CONV), (0, 0))),
        ssd_cb=vec(ssd_conv_b[0]),
        dtbias=pad_lanes(ssd_dt_bias[0].reshape(1, -1)),
        alog=pad_lanes(ssd_a_log[0].reshape(1, -1)),
        dskip=vec(jnp.repeat(ssd_d[0], SSD_HEAD_DIM)),
        ssd_norm=vec(ssd_norm[0]),
        conf_cw=jnp.repeat(conf_conv_w[0], SUBLANE, axis=0),
        conf_cb=vec(conf_conv_b[0]), ln_g=vec(conf_ln_g[0]), ln_b=vec(conf_ln_b[0]),
        w_out=w_out[0].astype(BF16),
        fnet_wcs=_fnet_weights(fnet_w[0]), fnet_b=vec(fnet_b[0]),
        w_up=[ffn_w_up[i].astype(BF16).reshape(D_MODEL, 2 * _FF_NCH, _FF_CHUNK).transpose(1, 0, 2)
              for i in range(depth)],
        ffn_cw=[jnp.pad(ffn_conv_w[i].reshape(9, 2 * _FF_NCH, _FF_CHUNK), ((0, 7), (0, 0), (0, 0))).transpose(1, 0, 2)
                for i in range(depth)],
        w_down=[ffn_w_down[i].astype(BF16).reshape(_FF_NCH, _FF_CHUNK, D_MODEL) for i in range(depth)],
        norm_f=vec(norm_f),
    )

    h0_ctx = jnp.zeros((nb_ctx, 2, D_SSD, SSD_STATE), F32)
    y_prompt, st = _trunk(x_prompt, mods, h0_ctx, False, False, p)
    new_state = st.reshape(nb_ctx, 1, 2, SSD_HEADS, SSD_HEAD_DIM, SSD_STATE).astype(x_prompt.dtype)

    h0_lat = state_ssd[:, 0].astype(F32).reshape(nb_lat, 2, D_SSD, SSD_STATE)
    y_sample, _ = _trunk(x_sample, mods, h0_lat, True, True, p)
    return (y_prompt, y_sample, new_state)
```

```python
import functools
import math

import numpy as np
import jax
import jax.numpy as jnp
from jax import lax
from jax.experimental import pallas as pl
from jax.experimental.pallas import tpu as pltpu

F32 = jnp.float32
BF16 = jnp.bfloat16

D_MODEL = 1024
GRID_W = 64
SSD_HEAD_DIM = 64
SSD_HEADS = 16
D_SSD = SSD_HEADS * SSD_HEAD_DIM
SSD_STATE = 128
SSD_GROUPS = 4
SSD_CONV = 5
SSD_CHUNK = 128
CONF_CH = D_MODEL
CONF_KERNEL = 31
FNET_GROUPS = 8
D_FF = 2816
N_MOD = 6
EPS = 1e-6
D_XBC = D_SSD + 2 * SSD_GROUPS * SSD_STATE

LANE = 128
SUBLANE = 8
VMEM_LIMIT = 56 * 1024 * 1024

MOD_ROWS = 16
CTX_ROW = 8


def _dot(a, b):
    return jnp.dot(a, b, preferred_element_type=F32)


def _silu(v):
    return v * jax.nn.sigmoid(v)


def _split2(v):
    hi = v.astype(BF16)
    lo = (v - hi.astype(F32)).astype(BF16)
    return hi, lo


def _split3(v):
    hi = v.astype(BF16)
    r = v - hi.astype(F32)
    mid = r.astype(BF16)
    lo = (r - mid.astype(F32)).astype(BF16)
    return hi, mid, lo


def _rms(x):
    return x * lax.rsqrt(jnp.mean(x * x, axis=-1, keepdims=True) + EPS)


def _modnorm(x, g, scale, shift):
    return (_rms(x) * g) * (1.0 + scale) + shift


def _cparams(sem, flags=None):
    return pltpu.CompilerParams(dimension_semantics=sem, vmem_limit_bytes=VMEM_LIMIT, flags=flags)


def _const_spec(shape, single=False):
    nd = len(shape)
    if single:
        return pl.BlockSpec(shape, lambda *_: (0,) * nd, pipeline_mode=pl.Buffered(1))
    return pl.BlockSpec(shape, lambda *_: (0,) * nd)


def _mod_kernel(cond_ref, w_ref, b_ref, o_ref):
    s = _silu(cond_ref[...])
    s_hi, s_lo = _split2(s)
    w_hi, w_lo = _split2(w_ref[0])
    acc = _dot(s_hi, w_hi) + _dot(s_hi, w_lo) + _dot(s_lo, w_hi)
    o_ref[0] = acc + b_ref[0]


def _modulation(cond, w_mod, b_mod):
    depth = w_mod.shape[0]
    n = w_mod.shape[2]
    tn = D_MODEL
    return pl.pallas_call(
        _mod_kernel,
        grid=(depth, n // tn),
        in_specs=[
            _const_spec((MOD_ROWS, D_MODEL)),
            pl.BlockSpec((1, D_MODEL, tn), lambda i, j: (i, 0, j)),
            pl.BlockSpec((1, 1, tn), lambda i, j: (i, 0, j)),
        ],
        out_specs=pl.BlockSpec((1, MOD_ROWS, tn), lambda i, j: (i, 0, j)),
        out_shape=jax.ShapeDtypeStruct((depth, MOD_ROWS, n), F32),
        compiler_params=_cparams(("parallel", "parallel")),
        name="modulation",
    )(cond, w_mod, b_mod.reshape(depth, 1, n))


def _mod_spec(seq_tiles, per_batch):
    if per_batch:
        return pl.BlockSpec((1, N_MOD, D_MODEL), lambda i: (i // seq_tiles, 0, 0))
    return pl.BlockSpec((1, N_MOD, D_MODEL), lambda i: (CTX_ROW, 0, 0))


_XHALO = 16
_XPAD = SUBLANE
_XSHIFTS = tuple(d for d in range(-(SSD_CONV // 2), SSD_CONV // 2 + 1) if d != 0)


def _inproj_kernel(x_ref, xp_ref, xn_ref, mod_ref, g_ref, wz_ref, wx_ref, wg_ref, wd_ref, cw_ref, cb_ref,
                   z_ref, xbc_ref, glu_ref, dt_ref, xe_ref, sh_ref, *, tm, seq_tiles):
    i = pl.program_id(0)
    has_prev = i % seq_tiles > 0
    has_next = i % seq_tiles < seq_tiles - 1
    m = mod_ref[0]
    g = g_ref[...]
    norm = lambda v: _modnorm(v, g, m[1:2], m[0:1]).astype(BF16)
    h = norm(x_ref[...])
    n_ext = tm + 2 * _XHALO

    hext = jnp.concatenate([norm(xp_ref[...]), h, norm(xn_ref[...])], axis=0)
    xe = _dot(hext, wx_ref[...])
    row = lax.broadcasted_iota(jnp.int32, (n_ext, 1), 0)
    inside = ((row >= _XHALO) | has_prev) & ((row < _XHALO + tm) | has_next)
    xe = jnp.where(inside, xe, 0.0)
    xe_ref[...] = xe
    for k, d in enumerate(_XSHIFTS):
        sh_ref[k, _XPAD - d:_XPAD - d + n_ext, :] = xe

    z_ref[...] = _dot(h, wz_ref[...])
    glu_ref[...] = _dot(h, wg_ref[...])
    dt_ref[...] = _dot(h, wd_ref[...])

    for rb in range(tm // (2 * SUBLANE)):
        base = rb * 2 * SUBLANE
        accs = [jnp.broadcast_to(cb_ref[...], (SUBLANE, D_XBC))] * 2
        for j in range(SSD_CONV):
            d = j - SSD_CONV // 2
            w8 = cw_ref[j * SUBLANE:(j + 1) * SUBLANE, :]
            for q in range(2):
                r0 = base + q * SUBLANE + _XHALO
                if d == 0:
                    v = xe_ref[pl.ds(r0, SUBLANE), :]
                else:
                    v = sh_ref[_XSHIFTS.index(d), pl.ds(r0 + _XPAD, SUBLANE), :]
                accs[q] = accs[q] + w8 * v
        for q in range(2):
            xbc_ref[pl.ds(base + q * SUBLANE, SUBLANE), :] = _silu(accs[q])


def _in_proj(x2, mod, g, wz, wx, wg, wd, cw, cb, seq, per_batch, tm):
    t = x2.shape[0]
    seq_tiles = seq // tm
    hb = tm // _XHALO
    last_hb = t // _XHALO - 1
    row = lambda w: pl.BlockSpec((tm, w), lambda i: (i, 0))
    n_ext = tm + 2 * _XHALO
    return pl.pallas_call(
        functools.partial(_inproj_kernel, tm=tm, seq_tiles=seq_tiles),
        grid=(t // tm,),
        in_specs=[row(D_MODEL),
                  pl.BlockSpec((_XHALO, D_MODEL), lambda i: (jnp.maximum(i * hb - 1, 0), 0)),
                  pl.BlockSpec((_XHALO, D_MODEL), lambda i: (jnp.minimum((i + 1) * hb, last_hb), 0)),
                  _mod_spec(seq_tiles, per_batch), _const_spec((1, D_MODEL)),
                  _const_spec(wz.shape), _const_spec(wx.shape), _const_spec(wg.shape),
                  _const_spec(wd.shape), _const_spec(cw.shape), _const_spec(cb.shape)],
        out_specs=[row(D_SSD), row(D_XBC), row(2 * CONF_CH), row(LANE)],
        out_shape=[jax.ShapeDtypeStruct((t, D_SSD), F32), jax.ShapeDtypeStruct((t, D_XBC), F32),
                   jax.ShapeDtypeStruct((t, 2 * CONF_CH), F32), jax.ShapeDtypeStruct((t, LANE), F32)],
        scratch_shapes=[pltpu.VMEM((n_ext, D_XBC), F32),
                        pltpu.VMEM((len(_XSHIFTS), n_ext + 2 * _XPAD, D_XBC), F32)],
        compiler_params=_cparams(("parallel",)),
        name="in_proj",
    )(x2, x2, x2, mod, g, wz, wx, wg, wd, cw, cb)


def _ssd_direction(xbc, dt_raw, dtb, a_row, ht_ref, d, rev):
    q = SSD_CHUNK
    xs = xbc[:, :D_SSD]
    bm = xbc[:, D_SSD:D_SSD + SSD_GROUPS * SSD_STATE]
    cm = xbc[:, D_SSD + SSD_GROUPS * SSD_STATE:]
    dt = jax.nn.softplus(dt_raw + dtb)
    la = dt * a_row

    ri = lax.broadcasted_iota(jnp.int32, (q, q), 0)
    ci = lax.broadcasted_iota(jnp.int32, (q, q), 1)
    keep = (ci >= ri) if rev else (ci <= ri)
    tri = jnp.where(keep, 1.0, 0.0).astype(BF16)
    tri_t = jnp.where((ri >= ci) if rev else (ri <= ci), 1.0, 0.0).astype(BF16)
    cum = sum(_dot(tri, p) for p in _split3(la))
    cum_t = sum(_dot(p, tri_t) for p in _split3(la.T))

    end = 0 if rev else q - 1
    cum_end = cum[end:end + 1, :]
    e_off = jnp.exp(cum)
    e_dte = jnp.exp(cum_end - cum)
    e_cd = jnp.exp(cum_end)

    er = lax.broadcasted_iota(jnp.int32, (LANE, D_SSD), 0)
    ec = lax.broadcasted_iota(jnp.int32, (LANE, D_SSD), 1)
    expand = jnp.where(er == ec // SSD_HEAD_DIM + d * SSD_HEADS, 1.0, 0.0).astype(BF16)

    def _expand(v):
        hi, lo = _split2(v)
        return _dot(hi, expand) + _dot(lo, expand)

    dt_x = _expand(dt)
    eoff_x = _expand(e_off)
    edte_x = _expand(e_dte)
    cd_x = _expand(jnp.broadcast_to(e_cd, (SUBLANE, LANE)))[0:1, :]

    xd = xs * dt_x
    lane = lax.broadcasted_iota(jnp.int32, (q, D_SSD), 1)
    even_head = (lane // SSD_HEAD_DIM) % 2 == 0
    xd_even = jnp.where(even_head, xd, 0.0).astype(BF16)
    xd_odd = jnp.where(even_head, 0.0, xd).astype(BF16)
    xdte = (xd * edte_x).astype(BF16)
    cb16 = cm.astype(BF16)

    gw = SSD_HEADS // SSD_GROUPS * SSD_HEAD_DIM
    ys = []
    for g in range(SSD_GROUPS):
        cg = cb16[:, g * SSD_STATE:(g + 1) * SSD_STATE]
        bg_t = bm[:, g * SSD_STATE:(g + 1) * SSD_STATE].T.astype(BF16)
        cbm = _dot(cg, bg_t)
        ht_g = ht_ref[d, :, g * gw:(g + 1) * gw]
        y_g = _dot(cg, ht_g.astype(BF16)) * eoff_x[:, g * gw:(g + 1) * gw]
        halves = []
        for pr in range(2):
            acc = None
            for r in range(2):
                h = g * 4 + pr * 2 + r
                c = d * SSD_HEADS + h
                seg = cum[:, c:c + 1] - cum_t[c:c + 1, :]
                decay = jnp.exp(jnp.where(keep, seg, -jnp.inf))
                m = (cbm * decay).astype(BF16)
                src = xd_even if r == 0 else xd_odd
                blk = (h // 2) * LANE
                part = _dot(m, src[:, blk:blk + LANE])
                acc = part if acc is None else acc + part
            halves.append(acc)
        ys.append(y_g + jnp.concatenate(halves, axis=1))
        s_t = _dot(bg_t, xdte[:, g * gw:(g + 1) * gw])
        ht_ref[d, :, g * gw:(g + 1) * gw] = ht_g * cd_x[:, g * gw:(g + 1) * gw] + s_t
    return jnp.concatenate(ys, axis=1), xs


def _ssd_kernel(xf_ref, xb_ref, dtf_ref, dtb_ref, h0_ref, dtbias_ref, alog_ref, dskip_ref,
                yf_ref, yb_ref, *rest, nc, emit_state):
    if emit_state:
        st_ref, ht_ref = rest
    else:
        (ht_ref,) = rest
    k = pl.program_id(1)

    @pl.when(k == 0)
    def _():
        for d in range(2):
            ht_ref[d] = h0_ref[0, d].T

    a_row = -jnp.exp(alog_ref[...])
    dtbias = dtbias_ref[...]

    y, xs = _ssd_direction(xf_ref[...], dtf_ref[...], dtbias, a_row, ht_ref, 0, False)
    yf_ref[...] = y + xs * dskip_ref[...]

    y, _ = _ssd_direction(xb_ref[...], dtb_ref[...], dtbias, a_row, ht_ref, 1, True)
    yb_ref[...] = y

    if emit_state:
        @pl.when(k == nc - 1)
        def _():
            for d in range(2):
                st_ref[0, d] = ht_ref[d].T


def _ssd_scan(xbc, dt, h0t, dtbias, alog, dskip, nb, seq, emit_state):
    t = xbc.shape[0]
    q = SSD_CHUNK
    nc = seq // q

    def fwd(b, k):
        return b * nc + k

    def bwd(b, k):
        return b * nc + nc - 1 - k

    in_specs = [
        pl.BlockSpec((q, D_XBC), lambda b, k: (fwd(b, k), 0)),
        pl.BlockSpec((q, D_XBC), lambda b, k: (bwd(b, k), 0)),
        pl.BlockSpec((q, LANE), lambda b, k: (fwd(b, k), 0)),
        pl.BlockSpec((q, LANE), lambda b, k: (bwd(b, k), 0)),
        pl.BlockSpec((1, 2, D_SSD, SSD_STATE), lambda b, k: (b, 0, 0, 0)),
        _const_spec(dtbias.shape), _const_spec(alog.shape), _const_spec(dskip.shape)]
    out_specs = [pl.BlockSpec((q, D_SSD), lambda b, k: (fwd(b, k), 0)),
                 pl.BlockSpec((q, D_SSD), lambda b, k: (bwd(b, k), 0))]
    out_shape = [jax.ShapeDtypeStruct((t, D_SSD), F32), jax.ShapeDtypeStruct((t, D_SSD), F32)]
    if emit_state:
        out_specs.append(pl.BlockSpec((1, 2, D_SSD, SSD_STATE), lambda b, k: (b, 0, 0, 0)))
        out_shape.append(jax.ShapeDtypeStruct((nb, 2, D_SSD, SSD_STATE), F32))
    return pl.pallas_call(
        functools.partial(_ssd_kernel, nc=nc, emit_state=emit_state),
        grid=(nb, nc),
        in_specs=in_specs, out_specs=out_specs, out_shape=out_shape,
        scratch_shapes=[pltpu.VMEM((2, SSD_STATE, D_SSD), F32)],
        compiler_params=_cparams(("parallel", "arbitrary")),
        name="ssd_scan",
    )(xbc, xbc, dt, dt, h0t, dtbias, alog, dskip)


_CHALO = 16
_CPAD = CONF_KERNEL // 2
_ROWBLK = 2 * SUBLANE


def _mixout_kernel(x_ref, yf_ref, yb_ref, z_ref, glu_ref, glup_ref, glun_ref, mod_ref,
                   nrm_ref, cw_ref, cb_ref, lng_ref, lnb_ref, wo_ref, gffn_ref,
                   x1_ref, hf_ref, ext_ref, sh_ref, conv_ref, *, tm, seq_tiles):
    i = pl.program_id(0)
    has_prev = i % seq_tiles > 0
    has_next = i % seq_tiles < seq_tiles - 1

    def glu(ref):
        v = ref[...]
        return v[:, :CONF_CH] * jax.nn.sigmoid(v[:, CONF_CH:])

    n_ext = tm + 2 * _CHALO
    ext_ref[0:_CHALO, :] = jnp.where(has_prev, glu(glup_ref), 0.0)
    ext_ref[_CHALO:_CHALO + tm, :] = glu(glu_ref)
    ext_ref[_CHALO + tm:n_ext, :] = jnp.where(has_next, glu(glun_ref), 0.0)
    e = ext_ref[...]
    for r in range(SUBLANE):
        sh_ref[r, SUBLANE - r:SUBLANE - r + n_ext, :] = e

    def row_block(rb, carry):
        base = pl.multiple_of(rb * _ROWBLK, _ROWBLK)
        nsub = _ROWBLK // SUBLANE
        accs = [jnp.broadcast_to(cb_ref[...], (SUBLANE, CONF_CH))] * nsub
        for j in range(CONF_KERNEL):
            off = _CHALO - _CPAD + j
            qq, r = divmod(off, SUBLANE)
            w8 = cw_ref[j * SUBLANE:(j + 1) * SUBLANE, :]
            accs = [a + w8 * sh_ref[r, pl.ds(base + (qq + 1 + k) * SUBLANE, SUBLANE), :]
                    for k, a in enumerate(accs)]
        for k, a in enumerate(accs):
            conv_ref[pl.ds(base + k * SUBLANE, SUBLANE), :] = a
        return carry

    lax.fori_loop(0, tm // _ROWBLK, row_block, 0)

    u = conv_ref[...]
    mu = jnp.mean(u, axis=-1, keepdims=True)
    uc = u - mu
    var = jnp.mean(uc * uc, axis=-1, keepdims=True)
    u = _silu(uc * lax.rsqrt(var + EPS) * lng_ref[...] + lnb_ref[...])

    y = (yf_ref[...] + yb_ref[...]) * _silu(z_ref[...])
    y = _rms(y) * nrm_ref[...]

    out = _dot(y.astype(BF16), wo_ref[0:D_SSD, :]) + _dot(u.astype(BF16), wo_ref[D_SSD:, :])
    m = mod_ref[0]
    x1 = x_ref[...] + m[2:3] * out
    x1_ref[...] = x1
    hf_ref[...] = _modnorm(x1, gffn_ref[...], m[4:5], m[3:4]).astype(BF16)


def _mix_out(x2, yf, yb, z, glu, mod, nrm, cw, cb, lng, lnb, wo, gffn, seq, per_batch, tm):
    t = x2.shape[0]
    seq_tiles = seq // tm
    hb = tm // _CHALO
    last_hb = t // _CHALO - 1
    row = lambda w: pl.BlockSpec((tm, w), lambda i: (i, 0))
    vec = _const_spec((1, D_MODEL))
    return pl.pallas_call(
        functools.partial(_mixout_kernel, tm=tm, seq_tiles=seq_tiles),
        grid=(t // tm,),
        in_specs=[row(D_MODEL), row(D_SSD), row(D_SSD), row(D_SSD), row(2 * CONF_CH),
                  pl.BlockSpec((_CHALO, 2 * CONF_CH), lambda i: (jnp.maximum(i * hb - 1, 0), 0)),
                  pl.BlockSpec((_CHALO, 2 * CONF_CH), lambda i: (jnp.minimum((i + 1) * hb, last_hb), 0)),
                  _mod_spec(seq_tiles, per_batch),
                  vec, _const_spec(cw.shape), vec, vec, vec, _const_spec(wo.shape), vec],
        out_specs=[row(D_MODEL), row(D_MODEL)],
        out_shape=[jax.ShapeDtypeStruct((t, D_MODEL), F32), jax.ShapeDtypeStruct((t, D_MODEL), BF16)],
        scratch_shapes=[pltpu.VMEM((tm + 2 * _CHALO, CONF_CH), F32),
                        pltpu.VMEM((SUBLANE, tm + 2 * _CHALO + SUBLANE, CONF_CH), F32),
                        pltpu.VMEM((tm, CONF_CH), F32)],
        compiler_params=_cparams(("parallel",)),
        name="mix_out",
    )(x2, yf, yb, z, glu, glu, glu, mod, nrm, cw, cb, lng, lnb, wo, gffn)


_FF_CHUNK = 256
_FF_NCH = D_FF // _FF_CHUNK
_FHALO = 128
_FF_ROWBLK = 32
_FF_PAD = SUBLANE
_FFN_FLAGS = None


def _ffn_kernel(*refs, tm, seq_tiles, period, row_taps, halo, final_norm):
    refs = list(refs)
    hf_ref = refs.pop(0)
    if halo:
        hfp_ref, hfn_ref = refs.pop(0), refs.pop(0)
    x1_ref, mod_ref, wup_ref, cw_ref, wdn_ref = [refs.pop(0) for _ in range(5)]
    if final_norm:
        nf_ref, o_ref = refs.pop(0), refs.pop(0)
    else:
        modn_ref, gn_ref, o_ref, hn_ref = [refs.pop(0) for _ in range(4)]
    ext_ref = refs.pop(0) if halo else hf_ref
    sl_ref, sc_ref, sr_ref, ml_ref, mr_ref, g_ref, acc_ref = refs
    i = pl.program_id(0)
    pad = _FF_PAD
    n_ext = tm + 2 * halo

    if halo:
        has_prev = i % seq_tiles > 0
        has_next = i % seq_tiles < seq_tiles - 1
        ext_ref[0:halo, :] = jnp.where(has_prev, hfp_ref[_FHALO - halo:_FHALO, :], 0.0).astype(BF16)
        ext_ref[halo:halo + tm, :] = hf_ref[...]
        ext_ref[halo + tm:n_ext, :] = jnp.where(has_next, hfn_ref[0:halo, :], 0.0).astype(BF16)

    col = (lax.broadcasted_iota(jnp.int32, (n_ext, _FF_CHUNK), 0) + (period - halo % period)) % period
    ml_ref[...] = jnp.where(col == period - 1, 0.0, 1.0)
    mr_ref[...] = jnp.where(col == 0, 0.0, 1.0)
    zeros_pad = jnp.zeros((2 * pad, _FF_CHUNK), F32)
    for ref in (sl_ref, sr_ref):
        for k in range(4):
            ref[k, 0:2 * pad, :] = zeros_pad
            ref[k, n_ext:n_ext + 2 * pad, :] = zeros_pad
    acc_ref[...] = jnp.zeros((tm, D_MODEL), F32)

    def up(c, slot):
        for half in range(2):
            u = _dot(ext_ref[...], wup_ref[half * _FF_NCH + c])
            k = slot * 2 + half
            sc_ref[k, pad:pad + n_ext, :] = u
            sl_ref[k, pad + 1:pad + 1 + n_ext, :] = u * ml_ref[...]
            sr_ref[k, pad - 1:pad - 1 + n_ext, :] = u * mr_ref[...]

    def conv(c, slot):
        w = [cw_ref[half * _FF_NCH + c] for half in range(2)]
        for rb in range(tm // _FF_ROWBLK):
            r0 = rb * _FF_ROWBLK
            outs = []
            for half in range(2):
                k = slot * 2 + half
                acc = None
                for dr in row_taps:
                    base = pad + halo + dr * period + r0
                    for dc, ref in enumerate((sl_ref, sc_ref, sr_ref)):
                        tap = (dr + 1) * 3 + dc
                        term = w[half][tap:tap + 1, :] * ref[k, pl.ds(base, _FF_ROWBLK), :]
                        acc = term if acc is None else acc + term
                outs.append(acc)
            g_ref[slot, r0:r0 + _FF_ROWBLK, :] = (outs[0] * _silu(outs[1])).astype(BF16)

    def down(c, slot):
        acc_ref[...] += _dot(g_ref[slot], wdn_ref[c])

    last = _FF_NCH - 1
    up(0, 0)
    up(1, 1)
    conv(0, 0)

    for c in range(1, last + 1):
        if c < last:
            up(c + 1, (c + 1) % 2)
        conv(c, c % 2)
        down(c - 1, (c - 1) % 2)
    down(last, last % 2)

    m = mod_ref[0]
    x2 = x1_ref[...] + m[5:6] * acc_ref[...]
    if final_norm:
        o_ref[...] = _rms(x2) * nf_ref[...]
    else:
        o_ref[...] = x2
        mn = modn_ref[0]
        hn_ref[...] = _modnorm(x2, gn_ref[...], mn[1:2], mn[0:1])


def _conv_ffn(hf, x1, mod, wup, cw, wdn, tail, seq, per_batch, tm, latent, final_norm):
    t = hf.shape[0]
    seq_tiles = seq // tm
    if latent:
        period, row_taps, halo = GRID_W, (-1, 0, 1), GRID_W + 16
    else:
        period, row_taps, halo = seq, (0,), 0
    n_ext = tm + 2 * halo
    row = lambda w: pl.BlockSpec((tm, w), lambda i: (i, 0))
    in_specs = [row(D_MODEL)]
    args = [hf]
    scratch = []
    if halo:
        hb = tm // _FHALO
        last_hb = t // _FHALO - 1
        in_specs += [pl.BlockSpec((_FHALO, D_MODEL), lambda i: (jnp.maximum(i * hb - 1, 0), 0)),
                     pl.BlockSpec((_FHALO, D_MODEL), lambda i: (jnp.minimum((i + 1) * hb, last_hb), 0))]
        args += [hf, hf]
        scratch.append(pltpu.VMEM((n_ext, D_MODEL), BF16))
    in_specs += [row(D_MODEL), _mod_spec(seq_tiles, per_batch), _const_spec(wup.shape, single=True),
                 _const_spec(cw.shape), _const_spec(wdn.shape, single=True)]
    args += [x1, mod, wup, cw, wdn]
    vec = _const_spec((1, D_MODEL))
    if final_norm:
        in_specs += [vec]
        out_specs = row(D_MODEL)
        out_shape = jax.ShapeDtypeStruct((t, D_MODEL), F32)
    else:
        in_specs += [_mod_spec(seq_tiles, per_batch), vec]
        out_specs = [row(D_MODEL), row(D_MODEL)]
        out_shape = [jax.ShapeDtypeStruct((t, D_MODEL), F32)] * 2
    args += list(tail)
    ubuf = pltpu.VMEM((4, n_ext + 2 * _FF_PAD, _FF_CHUNK), F32)
    mask = pltpu.VMEM((n_ext, _FF_CHUNK), F32)
    scratch += [ubuf, ubuf, ubuf, mask, mask,
                pltpu.VMEM((2, tm, _FF_CHUNK), BF16), pltpu.VMEM((tm, D_MODEL), F32)]
    return pl.pallas_call(
        functools.partial(_ffn_kernel, tm=tm, seq_tiles=seq_tiles, period=period,
                          row_taps=row_taps, halo=halo, final_norm=final_norm),
        grid=(t // tm,),
        in_specs=in_specs,
        out_specs=out_specs,
        out_shape=out_shape,
        scratch_shapes=scratch,
        compiler_params=_cparams(("parallel",), _FFN_FLAGS),
        name="conv_ffn",
    )(*args)


def _dft_tables(n):
    k = np.arange(n, dtype=np.int64)
    ang = 2.0 * np.pi * ((k[:, None] * k[None, :]) % n) / n
    s = 1.0 / math.sqrt(n)
    return np.cos(ang) * s, np.sin(ang) * s


def _seqdft_direct_kernel(h_ref, m_ref, pr_ref, pi_ref, *, seq):
    p = _dot(m_ref[...], h_ref[...].astype(BF16))
    pr_ref[...] = p[0:seq].astype(BF16)
    pi_ref[...] = p[seq:2 * seq].astype(BF16)


def _seq_dft_direct(h2, nb, seq):
    c, s = _dft_tables(seq)
    mat = jnp.asarray(np.concatenate([c, -s], axis=0), BF16)
    t = h2.shape[0]
    tc = 512
    blk = pl.BlockSpec((seq, tc), lambda b, j: (b, j))
    return pl.pallas_call(
        functools.partial(_seqdft_direct_kernel, seq=seq),
        grid=(nb, D_MODEL // tc),
        in_specs=[blk, _const_spec(mat.shape)],
        out_specs=[blk, blk],
        out_shape=[jax.ShapeDtypeStruct((t, D_MODEL), BF16)] * 2,
        compiler_params=_cparams(("parallel", "parallel")), name="seq_dft_direct",
    )(h2, mat)


_DFT_PITCH = GRID_W + 4
_DFT_TC = 2 * LANE


def _seqdft_2stage_kernel(h_ref, m1_ref, m2_ref, pr_ref, pi_ref, xa_ref, yr_ref, yi_ref, pb_ref, *, w):
    nlb = _DFT_TC // LANE
    pitch = _DFT_PITCH
    lanes = lambda v, j: v[:, j * LANE:(j + 1) * LANE]

    def gather(ref, r):
        return jnp.concatenate([ref[j, pl.ds(r, w, stride=pitch), :] for j in range(nlb)], axis=1)

    for n1 in range(w):
        rows = h_ref[n1 * w:(n1 + 1) * w, :]
        for j in range(nlb):
            xa_ref[j, n1 * pitch:n1 * pitch + w, :] = lanes(rows, j)
    for n2 in range(w):
        y = _dot(m1_ref[n2], gather(xa_ref, n2).astype(BF16))
        for j in range(nlb):
            yr_ref[j, n2 * pitch:n2 * pitch + w, :] = lanes(y[0:w], j)
            yi_ref[j, n2 * pitch:n2 * pitch + w, :] = lanes(y[w:2 * w], j)
    m2 = m2_ref[...]
    for k1 in range(w):
        yk = jnp.concatenate([gather(yr_ref, k1), gather(yi_ref, k1)], axis=0).astype(BF16)
        p = _dot(m2, yk)
        for j in range(nlb):
            xa_ref[j, pl.ds(k1, w, stride=pitch), :] = lanes(p[0:w], j)
            pb_ref[j, pl.ds(k1, w, stride=pitch), :] = lanes(p[w:2 * w], j)
    for k2 in range(w):
        for ref, out in ((xa_ref, pr_ref), (pb_ref, pi_ref)):
            out[k2 * w:(k2 + 1) * w, :] = jnp.concatenate(
                [ref[j, k2 * pitch:k2 * pitch + w, :] for j in range(nlb)], axis=1).astype(BF16)


def _seq_dft_2stage(h2, nb, seq):
    w = GRID_W
    assert seq == w * w
    k1 = np.arange(w, dtype=np.int64)[:, None]
    n1 = np.arange(w, dtype=np.int64)[None, :]
    m1 = np.zeros((w, 2 * w, w), np.float64)
    for n2 in range(w):
        ang = 2.0 * np.pi * ((k1 * (w * n1 + n2)) % seq) / seq
        m1[n2, :w] = np.cos(ang)
        m1[n2, w:] = -np.sin(ang)
    m1 /= math.sqrt(seq)
    ang2 = 2.0 * np.pi * ((k1 * n1) % w) / w
    c2, s2 = np.cos(ang2), np.sin(ang2)
    m2 = np.block([[c2, s2], [-s2, c2]])
    t = h2.shape[0]
    tc = _DFT_TC
    blk = pl.BlockSpec((seq, tc), lambda b, j: (b, j))
    pitched = pltpu.VMEM((tc // LANE, w * _DFT_PITCH, LANE), F32)
    return pl.pallas_call(
        functools.partial(_seqdft_2stage_kernel, w=w),
        grid=(nb, D_MODEL // tc),
        in_specs=[blk, _const_spec(m1.shape), _const_spec(m2.shape)],
        out_specs=[blk, blk],
        out_shape=[jax.ShapeDtypeStruct((t, D_MODEL), BF16)] * 2,
        scratch_shapes=[pitched] * 4,
        compiler_params=_cparams(("parallel", "parallel")), name="seq_dft_2stage",
    )(h2, jnp.asarray(m1, BF16), jnp.asarray(m2, BF16))


def _fnetw_kernel(dft_ref, w_ref, o_ref):
    o_ref[...] = _dot(dft_ref[...], w_ref[...].astype(BF16)).astype(BF16)


def _fnet_weights(fnet_w):
    gsz = D_MODEL // FNET_GROUPS
    c, s = _dft_tables(gsz)
    eye = np.eye(FNET_GROUPS)
    dft = jnp.asarray(np.concatenate([np.kron(eye, c), np.kron(eye, s)], axis=0), BF16)
    tmw = 512
    return pl.pallas_call(
        _fnetw_kernel, grid=(2 * D_MODEL // tmw,),
        in_specs=[pl.BlockSpec((tmw, D_MODEL), lambda i: (i, 0)), _const_spec((D_MODEL, D_MODEL))],
        out_specs=pl.BlockSpec((tmw, D_MODEL), lambda i: (i, 0)),
        out_shape=jax.ShapeDtypeStruct((2 * D_MODEL, D_MODEL), BF16),
        compiler_params=_cparams(("parallel",)), name="fnet_weights",
    )(dft, fnet_w)


def _fnetout_kernel(pr_ref, pi_ref, x_ref, mod_ref, w_ref, b_ref, gffn_ref, x1_ref, hf_ref):
    out = _dot(pr_ref[...], w_ref[0:D_MODEL, :]) + _dot(pi_ref[...], w_ref[D_MODEL:, :]) + b_ref[...]
    m = mod_ref[0]
    x1 = x_ref[...] + m[2:3] * out
    x1_ref[...] = x1
    hf_ref[...] = _modnorm(x1, gffn_ref[...], m[4:5], m[3:4]).astype(BF16)


def _fnet_out(pr, pi, x2, mod, wcs, b, gffn, seq, per_batch, tm):
    t = x2.shape[0]
    row = pl.BlockSpec((tm, D_MODEL), lambda i: (i, 0))
    vec = _const_spec((1, D_MODEL))
    return pl.pallas_call(
        _fnetout_kernel, grid=(t // tm,),
        in_specs=[row, row, row, _mod_spec(seq // tm, per_batch), _const_spec(wcs.shape), vec, vec],
        out_specs=[row, row],
        out_shape=[jax.ShapeDtypeStruct((t, D_MODEL), F32), jax.ShapeDtypeStruct((t, D_MODEL), BF16)],
        compiler_params=_cparams(("parallel",)), name="fnet_out",
    )(pr, pi, x2, mod, wcs, b, gffn)


def _trunk(x, mods, h0t, per_batch, latent, p):
    nb, seq, _ = x.shape
    t = nb * seq
    x2 = x.reshape(t, D_MODEL)
    tm = 256

    z, xbc, glu, dt = _in_proj(x2, mods[0], p['g_mix'][0], p['wz'], p['wx'], p['wg'], p['wd'],
                               p['ssd_cw'], p['ssd_cb'], seq, per_batch, tm)
    scan = _ssd_scan(xbc, dt, h0t, p['dtbias'], p['alog'], p['dskip'], nb, seq, emit_state=not latent)
    yf, yb = scan[0], scan[1]
    state = scan[2] if not latent else None
    x1, hf = _mix_out(x2, yf, yb, z, glu, mods[0], p['ssd_norm'], p['conf_cw'], p['conf_cb'],
                      p['ln_g'], p['ln_b'], p['w_out'], p['g_ffn'][0], seq, per_batch, tm)
    ftm = 512
    x2, hm = _conv_ffn(hf, x1, mods[0], p['w_up'][0], p['ffn_cw'][0], p['w_down'][0],
                       (mods[1], p['g_mix'][1]), seq, per_batch, ftm, latent, final_norm=False)

    if seq == GRID_W * GRID_W:
        pr, pi = _seq_dft_2stage(hm, nb, seq)
    else:
        pr, pi = _seq_dft_direct(hm, nb, seq)
    x1, hf = _fnet_out(pr, pi, x2, mods[1], p['fnet_wcs'], p['fnet_b'], p['g_ffn'][1],
                       seq, per_batch, tm)
    y = _conv_ffn(hf, x1, mods[1], p['w_up'][1], p['ffn_cw'][1], p['w_down'][1], (p['norm_f'],),
                  seq, per_batch, ftm, latent, final_norm=True)
    return y.reshape(nb, seq, D_MODEL), state


def kernel(x_prompt, x_sample, c, state_ssd, c_ctx, w_mod, b_mod, g_mix, g_ffn, w_in, ssd_conv_w, ssd_conv_b, ssd_dt_bias, ssd_a_log, ssd_d, ssd_norm, conf_conv_w, conf_conv_b, conf_ln_g, conf_ln_b, w_out, fnet_w, fnet_b, ffn_w_up, ffn_conv_w, ffn_w_down, norm_f):
    depth = w_mod.shape[0]
    nb_ctx = x_prompt.shape[0]
    nb_lat = x_sample.shape[0]
    assert nb_lat <= CTX_ROW

    cond = jnp.zeros((MOD_ROWS, D_MODEL), F32).at[:nb_lat].set(c).at[CTX_ROW].set(c_ctx)
    mod = _modulation(cond, w_mod, b_mod).reshape(depth, MOD_ROWS, N_MOD, D_MODEL)
    mods = [mod[i] for i in range(depth)]

    def pad_lanes(v, fill=0.0):
        return jnp.pad(v, ((0, 0), (0, LANE - v.shape[1])), constant_values=fill)

    w_in0 = w_in[0].astype(BF16)
    o_dt = D_SSD + D_XBC
    o_glu = o_dt + 2 * SSD_HEADS
    vec = lambda v: v.reshape(1, -1)
    p = dict(
        g_mix=[vec(g_mix[i]) for i in range(depth)],
        g_ffn=[vec(g_ffn[i]) for i in range(depth)],
        wz=w_in0[:, :D_SSD], wx=w_in0[:, D_SSD:o_dt], wd=pad_lanes(w_in0[:, o_dt:o_glu]), wg=w_in0[:, o_glu:],
        ssd_cw=jnp.repeat(ssd_conv_w[0], SUBLANE, axis=0),
        ssd_cb=vec(ssd_conv_b[0]),
        dtbias=pad_lanes(ssd_dt_bias[0].reshape(1, -1)),
        alog=pad_lanes(ssd_a_log[0].reshape(1, -1)),
        dskip=vec(jnp.repeat(ssd_d[0], SSD_HEAD_DIM)),
        ssd_norm=vec(ssd_norm[0]),
        conf_cw=jnp.repeat(conf_conv_w[0], SUBLANE, axis=0),
        conf_cb=vec(conf_conv_b[0]), ln_g=vec(conf_ln_g[0]), ln_b=vec(conf_ln_b[0]),
        w_out=w_out[0].astype(BF16),
        fnet_wcs=_fnet_weights(fnet_w[0]), fnet_b=vec(fnet_b[0]),
        w_up=[ffn_w_up[i].astype(BF16).reshape(D_MODEL, 2 * _FF_NCH, _FF_CHUNK).transpose(1, 0, 2)
              for i in range(depth)],
        ffn_cw=[jnp.pad(ffn_conv_w[i].reshape(9, 2 * _FF_NCH, _FF_CHUNK), ((0, 7), (0, 0), (0, 0))).transpose(1, 0, 2)
                for i in range(depth)],
        w_down=[ffn_w_down[i].astype(BF16).reshape(_FF_NCH, _FF_CHUNK, D_MODEL) for i in range(depth)],
        norm_f=vec(norm_f),
    )

    h0_ctx = jnp.zeros((nb_ctx, 2, D_SSD, SSD_STATE), F32)
    y_prompt, st = _trunk(x_prompt, mods, h0_ctx, False, False, p)
    new_state = st.reshape(nb_ctx, 1, 2, SSD_HEADS, SSD_HEAD_DIM, SSD_STATE).astype(x_prompt.dtype)

    h0_lat = state_ssd[:, 0].astype(F32).reshape(nb_lat, 2, D_SSD, SSD_STATE)
    y_sample, _ = _trunk(x_sample, mods, h0_lat, True, True, p)
    return (y_prompt, y_sample, new_state)
```

```python
import functools
import math

import numpy as np
import jax
import jax.numpy as jnp
from jax import lax
from jax.experimental import pallas as pl
from jax.experimental.pallas import tpu as pltpu

F32 = jnp.float32
BF16 = jnp.bfloat16

D_MODEL = 1024
GRID_W = 64
SSD_HEAD_DIM = 64
SSD_HEADS = 16
D_SSD = SSD_HEADS * SSD_HEAD_DIM
SSD_STATE = 128
SSD_GROUPS = 4
SSD_CONV = 5
SSD_CHUNK = 128
CONF_CH = D_MODEL
CONF_KERNEL = 31
FNET_GROUPS = 8
D_FF = 2816
N_MOD = 6
EPS = 1e-6
D_XBC = D_SSD + 2 * SSD_GROUPS * SSD_STATE

LANE = 128
SUBLANE = 8
VMEM_LIMIT = 56 * 1024 * 1024

MOD_ROWS = 16
CTX_ROW = 8


def _dot(a, b):
    return jnp.dot(a, b, preferred_element_type=F32)


def _silu(v):
    return v * jax.nn.sigmoid(v)


def _split2(v):
    hi = v.astype(BF16)
    lo = (v - hi.astype(F32)).astype(BF16)
    return hi, lo


def _split3(v):
    hi = v.astype(BF16)
    r = v - hi.astype(F32)
    mid = r.astype(BF16)
    lo = (r - mid.astype(F32)).astype(BF16)
    return hi, mid, lo


def _rms(x):
    return x * lax.rsqrt(jnp.mean(x * x, axis=-1, keepdims=True) + EPS)


def _modnorm(x, g, scale, shift):
    return (_rms(x) * g) * (1.0 + scale) + shift


def _cparams(sem, flags=None):
    return pltpu.CompilerParams(dimension_semantics=sem, vmem_limit_bytes=VMEM_LIMIT, flags=flags)


def _const_spec(shape, single=False):
    nd = len(shape)
    if single:
        return pl.BlockSpec(shape, lambda *_: (0,) * nd, pipeline_mode=pl.Buffered(1))
    return pl.BlockSpec(shape, lambda *_: (0,) * nd)


def _mod_kernel(cond_ref, w_ref, b_ref, o_ref):
    s = _silu(cond_ref[...])
    s_hi, s_lo = _split2(s)
    w_hi, w_lo = _split2(w_ref[0])
    acc = _dot(s_hi, w_hi) + _dot(s_hi, w_lo) + _dot(s_lo, w_hi)
    o_ref[0] = acc + b_ref[0]


def _modulation(cond, w_mod, b_mod):
    depth = w_mod.shape[0]
    n = w_mod.shape[2]
    tn = D_MODEL
    return pl.pallas_call(
        _mod_kernel,
        grid=(depth, n // tn),
        in_specs=[
            _const_spec((MOD_ROWS, D_MODEL)),
            pl.BlockSpec((1, D_MODEL, tn), lambda i, j: (i, 0, j)),
            pl.BlockSpec((1, 1, tn), lambda i, j: (i, 0, j)),
        ],
        out_specs=pl.BlockSpec((1, MOD_ROWS, tn), lambda i, j: (i, 0, j)),
        out_shape=jax.ShapeDtypeStruct((depth, MOD_ROWS, n), F32),
        compiler_params=_cparams(("parallel", "parallel")),
        name="modulation",
    )(cond, w_mod, b_mod.reshape(depth, 1, n))


def _mod_spec(seq_tiles, per_batch):
    if per_batch:
        return pl.BlockSpec((1, N_MOD, D_MODEL), lambda i: (i // seq_tiles, 0, 0))
    return pl.BlockSpec((1, N_MOD, D_MODEL), lambda i: (CTX_ROW, 0, 0))


_XHALO = 16
_XPAD = SUBLANE
_XSHIFTS = tuple(d for d in range(-(SSD_CONV // 2), SSD_CONV // 2 + 1) if d != 0)


def _inproj_kernel(x_ref, xp_ref, xn_ref, mod_ref, g_ref, wz_ref, wx_ref, wg_ref, wd_ref, cw_ref, cb_ref,
                   z_ref, xbc_ref, glu_ref, dt_ref, xe_ref, sh_ref, *, tm, seq_tiles):
    i = pl.program_id(0)
    has_prev = i % seq_tiles > 0
    has_next = i % seq_tiles < seq_tiles - 1
    m = mod_ref[0]
    g = g_ref[...]
    norm = lambda v: _modnorm(v, g, m[1:2], m[0:1]).astype(BF16)
    h = norm(x_ref[...])
    n_ext = tm + 2 * _XHALO

    hext = jnp.concatenate([norm(xp_ref[...]), h, norm(xn_ref[...])], axis=0)
    xe = _dot(hext, wx_ref[...])
    row = lax.broadcasted_iota(jnp.int32, (n_ext, 1), 0)
    inside = ((row >= _XHALO) | has_prev) & ((row < _XHALO + tm) | has_next)
    xe = jnp.where(inside, xe, 0.0)
    xe_ref[...] = xe
    for k, d in enumerate(_XSHIFTS):
        sh_ref[k, _XPAD - d:_XPAD - d + n_ext, :] = xe

    z_ref[...] = _dot(h, wz_ref[...])
    glu_ref[...] = _dot(h, wg_ref[...])
    dt_ref[...] = _dot(h, wd_ref[...])

    for rb in range(tm // (2 * SUBLANE)):
        base = rb * 2 * SUBLANE
        accs = [jnp.broadcast_to(cb_ref[...], (SUBLANE, D_XBC))] * 2
        for j in range(SSD_CONV):
            d = j - SSD_CONV // 2
            w8 = cw_ref[j * SUBLANE:(j + 1) * SUBLANE, :]
            for q in range(2):
                r0 = base + q * SUBLANE + _XHALO
                if d == 0:
                    v = xe_ref[pl.ds(r0, SUBLANE), :]
                else:
                    v = sh_ref[_XSHIFTS.index(d), pl.ds(r0 + _XPAD, SUBLANE), :]
                accs[q] = accs[q] + w8 * v
        for q in range(2):
            xbc_ref[pl.ds(base + q * SUBLANE, SUBLANE), :] = _silu(accs[q])


def _in_proj(x2, mod, g, wz, wx, wg, wd, cw, cb, seq, per_batch, tm):
    t = x2.shape[0]
    seq_tiles = seq // tm
    hb = tm // _XHALO
    last_hb = t // _XHALO - 1
    row = lambda w: pl.BlockSpec((tm, w), lambda i: (i, 0))
    n_ext = tm + 2 * _XHALO
    return pl.pallas_call(
        functools.partial(_inproj_kernel, tm=tm, seq_tiles=seq_tiles),
        grid=(t // tm,),
        in_specs=[row(D_MODEL),
                  pl.BlockSpec((_XHALO, D_MODEL), lambda i: (jnp.maximum(i * hb - 1, 0), 0)),
                  pl.BlockSpec((_XHALO, D_MODEL), lambda i: (jnp.minimum((i + 1) * hb, last_hb), 0)),
                  _mod_spec(seq_tiles, per_batch), _const_spec((1, D_MODEL)),
                  _const_spec(wz.shape), _const_spec(wx.shape), _const_spec(wg.shape),
                  _const_spec(wd.shape), _const_spec(cw.shape), _const_spec(cb.shape)],
        out_specs=[row(D_SSD), row(D_XBC), row(2 * CONF_CH), row(LANE)],
        out_shape=[jax.ShapeDtypeStruct((t, D_SSD), F32), jax.ShapeDtypeStruct((t, D_XBC), F32),
                   jax.ShapeDtypeStruct((t, 2 * CONF_CH), F32), jax.ShapeDtypeStruct((t, LANE), F32)],
        scratch_shapes=[pltpu.VMEM((n_ext, D_XBC), F32),
                        pltpu.VMEM((len(_XSHIFTS), n_ext + 2 * _XPAD, D_XBC), F32)],
        compiler_params=_cparams(("parallel",)),
        name="in_proj",
    )(x2, x2, x2, mod, g, wz, wx, wg, wd, cw, cb)


def _ssd_direction(xbc, dt_raw, dtb, a_row, ht_ref, d, rev):
    q = SSD_CHUNK
    xs = xbc[:, :D_SSD]
    bm = xbc[:, D_SSD:D_SSD + SSD_GROUPS * SSD_STATE]
    cm = xbc[:, D_SSD + SSD_GROUPS * SSD_STATE:]
    dt = jax.nn.softplus(dt_raw + dtb)
    la = dt * a_row

    ri = lax.broadcasted_iota(jnp.int32, (q, q), 0)
    ci = lax.broadcasted_iota(jnp.int32, (q, q), 1)
    keep = (ci >= ri) if rev else (ci <= ri)
    tri = jnp.where(keep, 1.0, 0.0).astype(BF16)
    tri_t = jnp.where((ri >= ci) if rev else (ri <= ci), 1.0, 0.0).astype(BF16)
    cum = sum(_dot(tri, p) for p in _split3(la))
    cum_t = sum(_dot(p, tri_t) for p in _split3(la.T))

    end = 0 if rev else q - 1
    cum_end = cum[end:end + 1, :]
    e_off = jnp.exp(cum)
    e_dte = jnp.exp(cum_end - cum)
    e_cd = jnp.exp(cum_end)

    er = lax.broadcasted_iota(jnp.int32, (LANE, D_SSD), 0)
    ec = lax.broadcasted_iota(jnp.int32, (LANE, D_SSD), 1)
    expand = jnp.where(er == ec // SSD_HEAD_DIM + d * SSD_HEADS, 1.0, 0.0).astype(BF16)

    def _expand(v, exact):
        if not exact:
            return _dot(v.astype(BF16), expand)
        hi, lo = _split2(v)
        return _dot(hi, expand) + _dot(lo, expand)

    eoff_x = _expand(e_off, True)
    cd_x = _expand(jnp.broadcast_to(e_cd, (SUBLANE, LANE)), True)[0:1, :]

    xd = xs * _expand(dt, False)
    lane = lax.broadcasted_iota(jnp.int32, (q, D_SSD), 1)
    even_head = (lane // SSD_HEAD_DIM) % 2 == 0
    xd_even = jnp.where(even_head, xd, 0.0).astype(BF16)
    xd_odd = jnp.where(even_head, 0.0, xd).astype(BF16)
    xdte = (xs * _expand(dt * e_dte, False)).astype(BF16)
    cb16 = cm.astype(BF16)

    gw = SSD_HEADS // SSD_GROUPS * SSD_HEAD_DIM
    ys = []
    for g in range(SSD_GROUPS):
        cg = cb16[:, g * SSD_STATE:(g + 1) * SSD_STATE]
        bg_t = bm[:, g * SSD_STATE:(g + 1) * SSD_STATE].T.astype(BF16)
        cbm = _dot(cg, bg_t)
        ht_g = ht_ref[d, :, g * gw:(g + 1) * gw]
        y_g = _dot(cg, ht_g.astype(BF16)) * eoff_x[:, g * gw:(g + 1) * gw]
        halves = []
        for pr in range(2):
            acc = None
            for r in range(2):
                h = g * 4 + pr * 2 + r
                c = d * SSD_HEADS + h
                seg = cum[:, c:c + 1] - cum_t[c:c + 1, :]
                decay = jnp.exp(jnp.where(keep, seg, -jnp.inf))
                m = (cbm * decay).astype(BF16)
                src = xd_even if r == 0 else xd_odd
                blk = (h // 2) * LANE
                part = _dot(m, src[:, blk:blk + LANE])
                acc = part if acc is None else acc + part
            halves.append(acc)
        ys.append(y_g + jnp.concatenate(halves, axis=1))
        s_t = _dot(bg_t, xdte[:, g * gw:(g + 1) * gw])
        ht_ref[d, :, g * gw:(g + 1) * gw] = ht_g * cd_x[:, g * gw:(g + 1) * gw] + s_t
    return jnp.concatenate(ys, axis=1), xs


def _ssd_kernel(xf_ref, xb_ref, dtf_ref, dtb_ref, h0_ref, dtbias_ref, alog_ref, dskip_ref,
                yf_ref, yb_ref, *rest, nc, emit_state):
    if emit_state:
        st_ref, ht_ref = rest
    else:
        (ht_ref,) = rest
    k = pl.program_id(1)

    @pl.when(k == 0)
    def _():
        for d in range(2):
            ht_ref[d] = h0_ref[0, d].T

    a_row = -jnp.exp(alog_ref[...])
    dtbias = dtbias_ref[...]

    y, xs = _ssd_direction(xf_ref[...], dtf_ref[...], dtbias, a_row, ht_ref, 0, False)
    yf_ref[...] = y + xs * dskip_ref[...]

    y, _ = _ssd_direction(xb_ref[...], dtb_ref[...], dtbias, a_row, ht_ref, 1, True)
    yb_ref[...] = y

    if emit_state:
        @pl.when(k == nc - 1)
        def _():
            for d in range(2):
                st_ref[0, d] = ht_ref[d].T


def _ssd_scan(xbc, dt, h0t, dtbias, alog, dskip, nb, seq, emit_state):
    t = xbc.shape[0]
    q = SSD_CHUNK
    nc = seq // q

    def fwd(b, k):
        return b * nc + k

    def bwd(b, k):
        return b * nc + nc - 1 - k

    in_specs = [
        pl.BlockSpec((q, D_XBC), lambda b, k: (fwd(b, k), 0)),
        pl.BlockSpec((q, D_XBC), lambda b, k: (bwd(b, k), 0)),
        pl.BlockSpec((q, LANE), lambda b, k: (fwd(b, k), 0)),
        pl.BlockSpec((q, LANE), lambda b, k: (bwd(b, k), 0)),
        pl.BlockSpec((1, 2, D_SSD, SSD_STATE), lambda b, k: (b, 0, 0, 0)),
        _const_spec(dtbias.shape), _const_spec(alog.shape), _const_spec(dskip.shape)]
    out_specs = [pl.BlockSpec((q, D_SSD), lambda b, k: (fwd(b, k), 0)),
                 pl.BlockSpec((q, D_SSD), lambda b, k: (bwd(b, k), 0))]
    out_shape = [jax.ShapeDtypeStruct((t, D_SSD), F32), jax.ShapeDtypeStruct((t, D_SSD), F32)]
    if emit_state:
        out_specs.append(pl.BlockSpec((1, 2, D_SSD, SSD_STATE), lambda b, k: (b, 0, 0, 0)))
        out_shape.append(jax.ShapeDtypeStruct((nb, 2, D_SSD, SSD_STATE), F32))
    return pl.pallas_call(
        functools.partial(_ssd_kernel, nc=nc, emit_state=emit_state),
        grid=(nb, nc),
        in_specs=in_specs, out_specs=out_specs, out_shape=out_shape,
        scratch_shapes=[pltpu.VMEM((2, SSD_STATE, D_SSD), F32)],
        compiler_params=_cparams(("parallel", "arbitrary")),
        name="ssd_scan",
    )(xbc, xbc, dt, dt, h0t, dtbias, alog, dskip)


_CHALO = 16
_CPAD = CONF_KERNEL // 2
_ROWBLK = 2 * SUBLANE


def _mixout_kernel(x_ref, yf_ref, yb_ref, z_ref, glu_ref, glup_ref, glun_ref, mod_ref,
                   nrm_ref, cw_ref, cb_ref, lng_ref, lnb_ref, wo_ref, gffn_ref,
                   x1_ref, hf_ref, ext_ref, sh_ref, conv_ref, *, tm, seq_tiles):
    i = pl.program_id(0)
    has_prev = i % seq_tiles > 0
    has_next = i % seq_tiles < seq_tiles - 1

    def glu(ref):
        v = ref[...]
        return v[:, :CONF_CH] * jax.nn.sigmoid(v[:, CONF_CH:])

    n_ext = tm + 2 * _CHALO
    ext_ref[0:_CHALO, :] = jnp.where(has_prev, glu(glup_ref), 0.0)
    ext_ref[_CHALO:_CHALO + tm, :] = glu(glu_ref)
    ext_ref[_CHALO + tm:n_ext, :] = jnp.where(has_next, glu(glun_ref), 0.0)
    e = ext_ref[...]
    for r in range(SUBLANE):
        sh_ref[r, SUBLANE - r:SUBLANE - r + n_ext, :] = e

    def row_block(rb):
        base = rb * _ROWBLK
        nsub = _ROWBLK // SUBLANE
        accs = [jnp.broadcast_to(cb_ref[...], (SUBLANE, CONF_CH))] * nsub
        for j in range(CONF_KERNEL):
            off = _CHALO - _CPAD + j
            qq, r = divmod(off, SUBLANE)
            w8 = cw_ref[j * SUBLANE:(j + 1) * SUBLANE, :]
            accs = [a + w8 * sh_ref[r, pl.ds(base + (qq + 1 + k) * SUBLANE, SUBLANE), :]
                    for k, a in enumerate(accs)]
        for k, a in enumerate(accs):
            conv_ref[pl.ds(base + k * SUBLANE, SUBLANE), :] = a

    for rb in range(tm // _ROWBLK):
        row_block(rb)

    u = conv_ref[...]
    mu = jnp.mean(u, axis=-1, keepdims=True)
    uc = u - mu
    var = jnp.mean(uc * uc, axis=-1, keepdims=True)
    u = _silu(uc * lax.rsqrt(var + EPS) * lng_ref[...] + lnb_ref[...])

    y = (yf_ref[...] + yb_ref[...]) * _silu(z_ref[...])
    y = _rms(y) * nrm_ref[...]

    out = _dot(y.astype(BF16), wo_ref[0:D_SSD, :]) + _dot(u.astype(BF16), wo_ref[D_SSD:, :])
    m = mod_ref[0]
    x1 = x_ref[...] + m[2:3] * out
    x1_ref[...] = x1
    hf_ref[...] = _modnorm(x1, gffn_ref[...], m[4:5], m[3:4]).astype(BF16)


def _mix_out(x2, yf, yb, z, glu, mod, nrm, cw, cb, lng, lnb, wo, gffn, seq, per_batch, tm):
    t = x2.shape[0]
    seq_tiles = seq // tm
    hb = tm // _CHALO
    last_hb = t // _CHALO - 1
    row = lambda w: pl.BlockSpec((tm, w), lambda i: (i, 0))
    vec = _const_spec((1, D_MODEL))
    return pl.pallas_call(
        functools.partial(_mixout_kernel, tm=tm, seq_tiles=seq_tiles),
        grid=(t // tm,),
        in_specs=[row(D_MODEL), row(D_SSD), row(D_SSD), row(D_SSD), row(2 * CONF_CH),
                  pl.BlockSpec((_CHALO, 2 * CONF_CH), lambda i: (jnp.maximum(i * hb - 1, 0), 0)),
                  pl.BlockSpec((_CHALO, 2 * CONF_CH), lambda i: (jnp.minimum((i + 1) * hb, last_hb), 0)),
                  _mod_spec(seq_tiles, per_batch),
                  vec, _const_spec(cw.shape), vec, vec, vec, _const_spec(wo.shape), vec],
        out_specs=[row(D_MODEL), row(D_MODEL)],
        out_shape=[jax.ShapeDtypeStruct((t, D_MODEL), F32), jax.ShapeDtypeStruct((t, D_MODEL), BF16)],
        scratch_shapes=[pltpu.VMEM((tm + 2 * _CHALO, CONF_CH), F32),
                        pltpu.VMEM((SUBLANE, tm + 2 * _CHALO + SUBLANE, CONF_CH), F32),
                        pltpu.VMEM((tm, CONF_CH), F32)],
        compiler_params=_cparams(("parallel",)),
        name="mix_out",
    )(x2, yf, yb, z, glu, glu, glu, mod, nrm, cw, cb, lng, lnb, wo, gffn)


_FF_CHUNK = 256
_FF_NCH = D_FF // _FF_CHUNK
_FHALO = 128
_FF_ROWBLK = 32
_FF_PAD = SUBLANE
_FFN_FLAGS = None


def _ffn_kernel(*refs, tm, seq_tiles, period, row_taps, halo, final_norm):
    refs = list(refs)
    hf_ref = refs.pop(0)
    if halo:
        hfp_ref, hfn_ref = refs.pop(0), refs.pop(0)
    x1_ref, mod_ref, wup_ref, cw_ref, wdn_ref = [refs.pop(0) for _ in range(5)]
    if final_norm:
        nf_ref, o_ref = refs.pop(0), refs.pop(0)
    else:
        modn_ref, gn_ref, o_ref, hn_ref = [refs.pop(0) for _ in range(4)]
    ext_ref = refs.pop(0) if halo else hf_ref
    sl_ref, sc_ref, sr_ref, ml_ref, mr_ref, g_ref, acc_ref = refs
    i = pl.program_id(0)
    pad = _FF_PAD
    n_ext = tm + 2 * halo

    if halo:
        has_prev = i % seq_tiles > 0
        has_next = i % seq_tiles < seq_tiles - 1
        ext_ref[0:halo, :] = jnp.where(has_prev, hfp_ref[_FHALO - halo:_FHALO, :], 0.0).astype(BF16)
        ext_ref[halo:halo + tm, :] = hf_ref[...]
        ext_ref[halo + tm:n_ext, :] = jnp.where(has_next, hfn_ref[0:halo, :], 0.0).astype(BF16)

    col = (lax.broadcasted_iota(jnp.int32, (n_ext, _FF_CHUNK), 0) + (period - halo % period)) % period
    ml_ref[...] = jnp.where(col == period - 1, 0.0, 1.0)
    mr_ref[...] = jnp.where(col == 0, 0.0, 1.0)
    zeros_pad = jnp.zeros((2 * pad, _FF_CHUNK), F32)
    for ref in (sl_ref, sr_ref):
        for k in range(4):
            ref[k, 0:2 * pad, :] = zeros_pad
            ref[k, n_ext:n_ext + 2 * pad, :] = zeros_pad
    acc_ref[...] = jnp.zeros((tm, D_MODEL), F32)

    def up(c, slot):
        for half in range(2):
            c0 = half * D_FF + c * _FF_CHUNK
            u = _dot(ext_ref[...], wup_ref[:, c0:c0 + _FF_CHUNK])
            k = slot * 2 + half
            sc_ref[k, pad:pad + n_ext, :] = u
            sl_ref[k, pad + 1:pad + 1 + n_ext, :] = u * ml_ref[...]
            sr_ref[k, pad - 1:pad - 1 + n_ext, :] = u * mr_ref[...]

    def conv(c, slot):
        w = [cw_ref[:, half * D_FF + c * _FF_CHUNK:half * D_FF + (c + 1) * _FF_CHUNK] for half in range(2)]
        for rb in range(tm // _FF_ROWBLK):
            r0 = rb * _FF_ROWBLK
            outs = []
            for half in range(2):
                k = slot * 2 + half
                acc = None
                for dr in row_taps:
                    base = pad + halo + dr * period + r0
                    for dc, ref in enumerate((sl_ref, sc_ref, sr_ref)):
                        tap = (dr + 1) * 3 + dc
                        term = w[half][tap:tap + 1, :] * ref[k, pl.ds(base, _FF_ROWBLK), :]
                        acc = term if acc is None else acc + term
                outs.append(acc)
            g_ref[slot, r0:r0 + _FF_ROWBLK, :] = (outs[0] * _silu(outs[1])).astype(BF16)

    def down(c, slot):
        acc_ref[...] += _dot(g_ref[slot], wdn_ref[c * _FF_CHUNK:(c + 1) * _FF_CHUNK, :])

    last = _FF_NCH - 1
    up(0, 0)
    up(1, 1)
    conv(0, 0)

    for c in range(1, last + 1):
        if c < last:
            up(c + 1, (c + 1) % 2)
        conv(c, c % 2)
        down(c - 1, (c - 1) % 2)
    down(last, last % 2)

    m = mod_ref[0]
    x2 = x1_ref[...] + m[5:6] * acc_ref[...]
    if final_norm:
        o_ref[...] = _rms(x2) * nf_ref[...]
    else:
        o_ref[...] = x2
        mn = modn_ref[0]
        hn_ref[...] = _modnorm(x2, gn_ref[...], mn[1:2], mn[0:1])


def _conv_ffn(hf, x1, mod, wup, cw, wdn, tail, seq, per_batch, tm, latent, final_norm):
    t = hf.shape[0]
    seq_tiles = seq // tm
    if latent:
        period, row_taps, halo = GRID_W, (-1, 0, 1), GRID_W + 16
    else:
        period, row_taps, halo = seq, (0,), 0
    n_ext = tm + 2 * halo
    row = lambda w: pl.BlockSpec((tm, w), lambda i: (i, 0))
    in_specs = [row(D_MODEL)]
    args = [hf]
    scratch = []
    if halo:
        hb = tm // _FHALO
        last_hb = t // _FHALO - 1
        in_specs += [pl.BlockSpec((_FHALO, D_MODEL), lambda i: (jnp.maximum(i * hb - 1, 0), 0)),
                     pl.BlockSpec((_FHALO, D_MODEL), lambda i: (jnp.minimum((i + 1) * hb, last_hb), 0))]
        args += [hf, hf]
        scratch.append(pltpu.VMEM((n_ext, D_MODEL), BF16))
    in_specs += [row(D_MODEL), _mod_spec(seq_tiles, per_batch), _const_spec(wup.shape, single=True),
                 _const_spec(cw.shape), _const_spec(wdn.shape, single=True)]
    args += [x1, mod, wup, cw, wdn]
    vec = _const_spec((1, D_MODEL))
    if final_norm:
        in_specs += [vec]
        out_specs = row(D_MODEL)
        out_shape = jax.ShapeDtypeStruct((t, D_MODEL), F32)
    else:
        in_specs += [_mod_spec(seq_tiles, per_batch), vec]
        out_specs = [row(D_MODEL), row(D_MODEL)]
        out_shape = [jax.ShapeDtypeStruct((t, D_MODEL), F32)] * 2
    args += list(tail)
    ubuf = pltpu.VMEM((4, n_ext + 2 * _FF_PAD, _FF_CHUNK), F32)
    mask = pltpu.VMEM((n_ext, _FF_CHUNK), F32)
    scratch += [ubuf, ubuf, ubuf, mask, mask,
                pltpu.VMEM((2, tm, _FF_CHUNK), BF16), pltpu.VMEM((tm, D_MODEL), F32)]
    return pl.pallas_call(
        functools.partial(_ffn_kernel, tm=tm, seq_tiles=seq_tiles, period=period,
                          row_taps=row_taps, halo=halo, final_norm=final_norm),
        grid=(t // tm,),
        in_specs=in_specs,
        out_specs=out_specs,
        out_shape=out_shape,
        scratch_shapes=scratch,
        compiler_params=_cparams(("parallel",), _FFN_FLAGS),
        name="conv_ffn",
    )(*args)


def _dft_tables(n):
    k = np.arange(n, dtype=np.int64)
    ang = 2.0 * np.pi * ((k[:, None] * k[None, :]) % n) / n
    s = 1.0 / math.sqrt(n)
    return np.cos(ang) * s, np.sin(ang) * s


def _seqdft_direct_kernel(h_ref, m_ref, pr_ref, pi_ref, *, seq):
    p = _dot(m_ref[...], h_ref[...].astype(BF16))
    pr_ref[...] = p[0:seq].astype(BF16)
    pi_ref[...] = p[seq:2 * seq].astype(BF16)


def _seq_dft_direct(h2, nb, seq):
    c, s = _dft_tables(seq)
    mat = jnp.asarray(np.concatenate([c, -s], axis=0), BF16)
    t = h2.shape[0]
    tc = 512
    blk = pl.BlockSpec((seq, tc), lambda b, j: (b, j))
    return pl.pallas_call(
        functools.partial(_seqdft_direct_kernel, seq=seq),
        grid=(nb, D_MODEL // tc),
        in_specs=[blk, _const_spec(mat.shape)],
        out_specs=[blk, blk],
        out_shape=[jax.ShapeDtypeStruct((t, D_MODEL), BF16)] * 2,
        compiler_params=_cparams(("parallel", "parallel")), name="seq_dft_direct",
    )(h2, mat)


_DFT_PITCH = GRID_W + 4
_DFT_TC = 2 * LANE


def _seqdft_2stage_kernel(h_ref, m1_ref, m2_ref, pr_ref, pi_ref, xa_ref, yr_ref, yi_ref, pb_ref, *, w):
    nlb = _DFT_TC // LANE
    pitch = _DFT_PITCH
    lanes = lambda v, j: v[:, j * LANE:(j + 1) * LANE]

    def gather(ref, r):
        return jnp.concatenate([ref[j, pl.ds(r, w, stride=pitch), :] for j in range(nlb)], axis=1)

    for n1 in range(w):
        rows = h_ref[n1 * w:(n1 + 1) * w, :]
        for j in range(nlb):
            xa_ref[j, n1 * pitch:n1 * pitch + w, :] = lanes(rows, j)
    for n2 in range(w):
        y = _dot(m1_ref[n2], gather(xa_ref, n2).astype(BF16))
        for j in range(nlb):
            yr_ref[j, n2 * pitch:n2 * pitch + w, :] = lanes(y[0:w], j)
            yi_ref[j, n2 * pitch:n2 * pitch + w, :] = lanes(y[w:2 * w], j)
    m2 = m2_ref[...]
    for k1 in range(w):
        yk = jnp.concatenate([gather(yr_ref, k1), gather(yi_ref, k1)], axis=0).astype(BF16)
        p = _dot(m2, yk)
        for j in range(nlb):
            xa_ref[j, pl.ds(k1, w, stride=pitch), :] = lanes(p[0:w], j)
            pb_ref[j, pl.ds(k1, w, stride=pitch), :] = lanes(p[w:2 * w], j)
    for k2 in range(w):
        for ref, out in ((xa_ref, pr_ref), (pb_ref, pi_ref)):
            out[k2 * w:(k2 + 1) * w, :] = jnp.concatenate(
                [ref[j, k2 * pitch:k2 * pitch + w, :] for j in range(nlb)], axis=1).astype(BF16)


def _seq_dft_2stage(h2, nb, seq):
    w = GRID_W
    assert seq == w * w
    k1 = np.arange(w, dtype=np.int64)[:, None]
    n1 = np.arange(w, dtype=np.int64)[None, :]
    m1 = np.zeros((w, 2 * w, w), np.float64)
    for n2 in range(w):
        ang = 2.0 * np.pi * ((k1 * (w * n1 + n2)) % seq) / seq
        m1[n2, :w] = np.cos(ang)
        m1[n2, w:] = -np.sin(ang)
    m1 /= math.sqrt(seq)
    ang2 = 2.0 * np.pi * ((k1 * n1) % w) / w
    c2, s2 = np.cos(ang2), np.sin(ang2)
    m2 = np.block([[c2, s2], [-s2, c2]])
    t = h2.shape[0]
    tc = _DFT_TC
    blk = pl.BlockSpec((seq, tc), lambda b, j: (b, j))
    pitched = pltpu.VMEM((tc // LANE, w * _DFT_PITCH, LANE), F32)
    return pl.pallas_call(
        functools.partial(_seqdft_2stage_kernel, w=w),
        grid=(nb, D_MODEL // tc),
        in_specs=[blk, _const_spec(m1.shape), _const_spec(m2.shape)],
        out_specs=[blk, blk],
        out_shape=[jax.ShapeDtypeStruct((t, D_MODEL), BF16)] * 2,
        scratch_shapes=[pitched] * 4,
        compiler_params=_cparams(("parallel", "parallel")), name="seq_dft_2stage",
    )(h2, jnp.asarray(m1, BF16), jnp.asarray(m2, BF16))


def _fnetw_kernel(dft_ref, w_ref, o_ref):
    o_ref[...] = _dot(dft_ref[...], w_ref[...].astype(BF16)).astype(BF16)


def _fnet_weights(fnet_w):
    gsz = D_MODEL // FNET_GROUPS
    c, s = _dft_tables(gsz)
    eye = np.eye(FNET_GROUPS)
    dft = jnp.asarray(np.concatenate([np.kron(eye, c), np.kron(eye, s)], axis=0), BF16)
    tmw = 512
    return pl.pallas_call(
        _fnetw_kernel, grid=(2 * D_MODEL // tmw,),
        in_specs=[pl.BlockSpec((tmw, D_MODEL), lambda i: (i, 0)), _const_spec((D_MODEL, D_MODEL))],
        out_specs=pl.BlockSpec((tmw, D_MODEL), lambda i: (i, 0)),
        out_shape=jax.ShapeDtypeStruct((2 * D_MODEL, D_MODEL), BF16),
        compiler_params=_cparams(("parallel",)), name="fnet_weights",
    )(dft, fnet_w)


def _fnetout_kernel(pr_ref, pi_ref, x_ref, mod_ref, w_ref, b_ref, gffn_ref, x1_ref, hf_ref):
    out = _dot(pr_ref[...], w_ref[0:D_MODEL, :]) + _dot(pi_ref[...], w_ref[D_MODEL:, :]) + b_ref[...]
    m = mod_ref[0]
    x1 = x_ref[...] + m[2:3] * out
    x1_ref[...] = x1
    hf_ref[...] = _modnorm(x1, gffn_ref[...], m[4:5], m[3:4]).astype(BF16)


def _fnet_out(pr, pi, x2, mod, wcs, b, gffn, seq, per_batch, tm):
    t = x2.shape[0]
    row = pl.BlockSpec((tm, D_MODEL), lambda i: (i, 0))
    vec = _const_spec((1, D_MODEL))
    return pl.pallas_call(
        _fnetout_kernel, grid=(t // tm,),
        in_specs=[row, row, row, _mod_spec(seq // tm, per_batch), _const_spec(wcs.shape), vec, vec],
        out_specs=[row, row],
        out_shape=[jax.ShapeDtypeStruct((t, D_MODEL), F32), jax.ShapeDtypeStruct((t, D_MODEL), BF16)],
        compiler_params=_cparams(("parallel",)), name="fnet_out",
    )(pr, pi, x2, mod, wcs, b, gffn)


def _trunk(x, mods, h0t, per_batch, latent, p):
    nb, seq, _ = x.shape
    t = nb * seq
    x2 = x.reshape(t, D_MODEL)
    tm = 256

    z, xbc, glu, dt = _in_proj(x2, mods[0], p['g_mix'][0], p['wz'], p['wx'], p['wg'], p['wd'],
                               p['ssd_cw'], p['ssd_cb'], seq, per_batch, tm)
    scan = _ssd_scan(xbc, dt, h0t, p['dtbias'], p['alog'], p['dskip'], nb, seq, emit_state=not latent)
    yf, yb = scan[0], scan[1]
    state = scan[2] if not latent else None
    x1, hf = _mix_out(x2, yf, yb, z, glu, mods[0], p['ssd_norm'], p['conf_cw'], p['conf_cb'],
                      p['ln_g'], p['ln_b'], p['w_out'], p['g_ffn'][0], seq, per_batch, tm)
    ftm = 512
    x2, hm = _conv_ffn(hf, x1, mods[0], p['w_up'][0], p['ffn_cw'][0], p['w_down'][0],
                       (mods[1], p['g_mix'][1]), seq, per_batch, ftm, latent, final_norm=False)

    if seq == GRID_W * GRID_W:
        pr, pi = _seq_dft_2stage(hm, nb, seq)
    else:
        pr, pi = _seq_dft_direct(hm, nb, seq)
    x1, hf = _fnet_out(pr, pi, x2, mods[1], p['fnet_wcs'], p['fnet_b'], p['g_ffn'][1],
                       seq, per_batch, tm if seq == tm else ftm)
    y = _conv_ffn(hf, x1, mods[1], p['w_up'][1], p['ffn_cw'][1], p['w_down'][1], (p['norm_f'],),
                  seq, per_batch, ftm, latent, final_norm=True)
    return y.reshape(nb, seq, D_MODEL), state


def kernel(x_prompt, x_sample, c, state_ssd, c_ctx, w_mod, b_mod, g_mix, g_ffn, w_in, ssd_conv_w, ssd_conv_b, ssd_dt_bias, ssd_a_log, ssd_d, ssd_norm, conf_conv_w, conf_conv_b, conf_ln_g, conf_ln_b, w_out, fnet_w, fnet_b, ffn_w_up, ffn_conv_w, ffn_w_down, norm_f):
    depth = w_mod.shape[0]
    nb_ctx = x_prompt.shape[0]
    nb_lat = x_sample.shape[0]
    assert nb_lat <= CTX_ROW

    cond = jnp.zeros((MOD_ROWS, D_MODEL), F32).at[:nb_lat].set(c).at[CTX_ROW].set(c_ctx)
    mod = _modulation(cond, w_mod, b_mod).reshape(depth, MOD_ROWS, N_MOD, D_MODEL)
    mods = [mod[i] for i in range(depth)]

    def pad_lanes(v, fill=0.0):
        return jnp.pad(v, ((0, 0), (0, LANE - v.shape[1])), constant_values=fill)

    w_in0 = w_in[0].astype(BF16)
    o_dt = D_SSD + D_XBC
    o_glu = o_dt + 2 * SSD_HEADS
    vec = lambda v: v.reshape(1, -1)
    p = dict(
        g_mix=[vec(g_mix[i]) for i in range(depth)],
        g_ffn=[vec(g_ffn[i]) for i in range(depth)],
        wz=w_in0[:, :D_SSD], wx=w_in0[:, D_SSD:o_dt], wd=pad_lanes(w_in0[:, o_dt:o_glu]), wg=w_in0[:, o_glu:],
        ssd_cw=jnp.repeat(ssd_conv_w[0], SUBLANE, axis=0),
        ssd_cb=vec(ssd_conv_b[0]),
        dtbias=pad_lanes(ssd_dt_bias[0].reshape(1, -1)),
        alog=pad_lanes(ssd_a_log[0].reshape(1, -1)),
        dskip=vec(jnp.repeat(ssd_d[0], SSD_HEAD_DIM)),
        ssd_norm=vec(ssd_norm[0]),
        conf_cw=jnp.repeat(conf_conv_w[0], SUBLANE, axis=0),
        conf_cb=vec(conf_conv_b[0]), ln_g=vec(conf_ln_g[0]), ln_b=vec(conf_ln_b[0]),
        w_out=w_out[0].astype(BF16),
        fnet_wcs=_fnet_weights(fnet_w[0]), fnet_b=vec(fnet_b[0]),
        w_up=[ffn_w_up[i].astype(BF16) for i in range(depth)],
        ffn_cw=[jnp.pad(ffn_conv_w[i].reshape(9, -1), ((0, 7), (0, 0))) for i in range(depth)],
        w_down=[ffn_w_down[i].astype(BF16) for i in range(depth)],
        norm_f=vec(norm_f),
    )

    h0_ctx = jnp.zeros((nb_ctx, 2, D_SSD, SSD_STATE), F32)
    y_prompt, st = _trunk(x_prompt, mods, h0_ctx, False, False, p)
    new_state = st.reshape(nb_ctx, 1, 2, SSD_HEADS, SSD_HEAD_DIM, SSD_STATE).astype(x_prompt.dtype)

    h0_lat = state_ssd[:, 0].astype(F32).reshape(nb_lat, 2, D_SSD, SSD_STATE)
    y_sample, _ = _trunk(x_sample, mods, h0_lat, True, True, p)
    return (y_prompt, y_sample, new_state)
```

```python
import functools
import math

import numpy as np
import jax
import jax.numpy as jnp
from jax import lax
from jax.experimental import pallas as pl
from jax.experimental.pallas import tpu as pltpu

F32 = jnp.float32
BF16 = jnp.bfloat16

D_MODEL = 1024
GRID_W = 64
SSD_HEAD_DIM = 64
SSD_HEADS = 16
D_SSD = SSD_HEADS * SSD_HEAD_DIM
SSD_STATE = 128
SSD_GROUPS = 4
SSD_CONV = 5
SSD_CHUNK = 256
CONF_CH = D_MODEL
CONF_KERNEL = 31
FNET_GROUPS = 8
D_FF = 2816
N_MOD = 6
EPS = 1e-6
D_XBC = D_SSD + 2 * SSD_GROUPS * SSD_STATE

LANE = 128
SUBLANE = 8
VMEM_LIMIT = 56 * 1024 * 1024

MOD_ROWS = 16
CTX_ROW = 8


def _dot(a, b):
    return jnp.dot(a, b, preferred_element_type=F32)


def _silu(v):
    return v * jax.nn.sigmoid(v)


def _split2(v):
    hi = v.astype(BF16)
    lo = (v - hi.astype(F32)).astype(BF16)
    return hi, lo


def _split3(v):
    hi = v.astype(BF16)
    r = v - hi.astype(F32)
    mid = r.astype(BF16)
    lo = (r - mid.astype(F32)).astype(BF16)
    return hi, mid, lo


def _rms(x):
    return x * lax.rsqrt(jnp.mean(x * x, axis=-1, keepdims=True) + EPS)


def _modnorm(x, g, scale, shift):
    return (_rms(x) * g) * (1.0 + scale) + shift


def _cparams(sem, flags=None):
    return pltpu.CompilerParams(dimension_semantics=sem, vmem_limit_bytes=VMEM_LIMIT, flags=flags)


def _const_spec(shape, single=False):
    nd = len(shape)
    if single:
        return pl.BlockSpec(shape, lambda *_: (0,) * nd, pipeline_mode=pl.Buffered(1))
    return pl.BlockSpec(shape, lambda *_: (0,) * nd)


def _mod_kernel(cond_ref, w_ref, b_ref, o_ref):
    s = _silu(cond_ref[...])
    s_hi, s_lo = _split2(s)
    w_hi, w_lo = _split2(w_ref[0])
    acc = _dot(s_hi, w_hi) + _dot(s_hi, w_lo) + _dot(s_lo, w_hi)
    o_ref[0] = acc + b_ref[0]


def _modulation(cond, w_mod, b_mod):
    depth = w_mod.shape[0]
    n = w_mod.shape[2]
    tn = D_MODEL
    return pl.pallas_call(
        _mod_kernel,
        grid=(depth, n // tn),
        in_specs=[
            _const_spec((MOD_ROWS, D_MODEL)),
            pl.BlockSpec((1, D_MODEL, tn), lambda i, j: (i, 0, j)),
            pl.BlockSpec((1, 1, tn), lambda i, j: (i, 0, j)),
        ],
        out_specs=pl.BlockSpec((1, MOD_ROWS, tn), lambda i, j: (i, 0, j)),
        out_shape=jax.ShapeDtypeStruct((depth, MOD_ROWS, n), F32),
        compiler_params=_cparams(("parallel", "parallel")),
        name="modulation",
    )(cond, w_mod, b_mod.reshape(depth, 1, n))


def _mod_spec(seq_tiles, per_batch):
    if per_batch:
        return pl.BlockSpec((1, N_MOD, D_MODEL), lambda i: (i // seq_tiles, 0, 0))
    return pl.BlockSpec((1, N_MOD, D_MODEL), lambda i: (CTX_ROW, 0, 0))


_XHALO = 16
_XPAD = SUBLANE
_XSHIFTS = tuple(d for d in range(-(SSD_CONV // 2), SSD_CONV // 2 + 1) if d != 0)


def _inproj_kernel(x_ref, xp_ref, xn_ref, mod_ref, g_ref, wz_ref, wx_ref, wg_ref, wd_ref, cw_ref, cb_ref,
                   z_ref, xbc_ref, glu_ref, dt_ref, xe_ref, sh_ref, *, tm, seq_tiles):
    i = pl.program_id(0)
    has_prev = i % seq_tiles > 0
    has_next = i % seq_tiles < seq_tiles - 1
    m = mod_ref[0]
    g = g_ref[...]
    norm = lambda v: _modnorm(v, g, m[1:2], m[0:1]).astype(BF16)
    h = norm(x_ref[...])
    n_ext = tm + 2 * _XHALO

    hext = jnp.concatenate([norm(xp_ref[...]), h, norm(xn_ref[...])], axis=0)
    xe = _dot(hext, wx_ref[...])
    row = lax.broadcasted_iota(jnp.int32, (n_ext, 1), 0)
    inside = ((row >= _XHALO) | has_prev) & ((row < _XHALO + tm) | has_next)
    xe = jnp.where(inside, xe, 0.0)
    xe_ref[...] = xe
    for k, d in enumerate(_XSHIFTS):
        sh_ref[k, _XPAD - d:_XPAD - d + n_ext, :] = xe

    z_ref[...] = _dot(h, wz_ref[...])
    glu_ref[...] = _dot(h, wg_ref[...])
    dt_ref[...] = _dot(h, wd_ref[...])

    for rb in range(tm // (2 * SUBLANE)):
        base = rb * 2 * SUBLANE
        accs = [jnp.broadcast_to(cb_ref[...], (SUBLANE, D_XBC))] * 2
        for j in range(SSD_CONV):
            d = j - SSD_CONV // 2
            w8 = cw_ref[j * SUBLANE:(j + 1) * SUBLANE, :]
            for q in range(2):
                r0 = base + q * SUBLANE + _XHALO
                if d == 0:
                    v = xe_ref[pl.ds(r0, SUBLANE), :]
                else:
                    v = sh_ref[_XSHIFTS.index(d), pl.ds(r0 + _XPAD, SUBLANE), :]
                accs[q] = accs[q] + w8 * v
        for q in range(2):
            xbc_ref[pl.ds(base + q * SUBLANE, SUBLANE), :] = _silu(accs[q])


def _in_proj(x2, mod, g, wz, wx, wg, wd, cw, cb, seq, per_batch, tm):
    t = x2.shape[0]
    seq_tiles = seq // tm
    hb = tm // _XHALO
    last_hb = t // _XHALO - 1
    row = lambda w: pl.BlockSpec((tm, w), lambda i: (i, 0))
    n_ext = tm + 2 * _XHALO
    return pl.pallas_call(
        functools.partial(_inproj_kernel, tm=tm, seq_tiles=seq_tiles),
        grid=(t // tm,),
        in_specs=[row(D_MODEL),
                  pl.BlockSpec((_XHALO, D_MODEL), lambda i: (jnp.maximum(i * hb - 1, 0), 0)),
                  pl.BlockSpec((_XHALO, D_MODEL), lambda i: (jnp.minimum((i + 1) * hb, last_hb), 0)),
                  _mod_spec(seq_tiles, per_batch), _const_spec((1, D_MODEL)),
                  _const_spec(wz.shape), _const_spec(wx.shape), _const_spec(wg.shape),
                  _const_spec(wd.shape), _const_spec(cw.shape), _const_spec(cb.shape)],
        out_specs=[row(D_SSD), row(D_XBC), row(2 * CONF_CH), row(LANE)],
        out_shape=[jax.ShapeDtypeStruct((t, D_SSD), F32), jax.ShapeDtypeStruct((t, D_XBC), F32),
                   jax.ShapeDtypeStruct((t, 2 * CONF_CH), F32), jax.ShapeDtypeStruct((t, LANE), F32)],
        scratch_shapes=[pltpu.VMEM((n_ext, D_XBC), F32),
                        pltpu.VMEM((len(_XSHIFTS), n_ext + 2 * _XPAD, D_XBC), F32)],
        compiler_params=_cparams(("parallel",)),
        name="in_proj",
    )(x2, x2, x2, mod, g, wz, wx, wg, wd, cw, cb)


def _ssd_direction(xbc, dt_raw, dtb, a_row, ht_ref, d, rev):
    q = SSD_CHUNK
    xs = xbc[:, :D_SSD]
    bm = xbc[:, D_SSD:D_SSD + SSD_GROUPS * SSD_STATE]
    cm = xbc[:, D_SSD + SSD_GROUPS * SSD_STATE:]
    dt = jax.nn.softplus(dt_raw + dtb)
    la = dt * a_row

    ri = lax.broadcasted_iota(jnp.int32, (q, q), 0)
    ci = lax.broadcasted_iota(jnp.int32, (q, q), 1)
    keep = (ci >= ri) if rev else (ci <= ri)
    tri = jnp.where(keep, 1.0, 0.0).astype(BF16)
    tri_t = jnp.where((ri >= ci) if rev else (ri <= ci), 1.0, 0.0).astype(BF16)
    cum = sum(_dot(tri, p) for p in _split3(la))
    cum_t = sum(_dot(p, tri_t) for p in _split3(la.T))

    end = 0 if rev else q - 1
    cum_end = cum[end:end + 1, :]
    e_off = jnp.exp(cum)
    e_dte = jnp.exp(cum_end - cum)
    e_cd = jnp.exp(cum_end)

    er = lax.broadcasted_iota(jnp.int32, (LANE, D_SSD), 0)
    ec = lax.broadcasted_iota(jnp.int32, (LANE, D_SSD), 1)
    expand = jnp.where(er == ec // SSD_HEAD_DIM + d * SSD_HEADS, 1.0, 0.0).astype(BF16)

    def _expand(v, exact):
        if not exact:
            return _dot(v.astype(BF16), expand)
        hi, lo = _split2(v)
        return _dot(hi, expand) + _dot(lo, expand)

    eoff_x = _expand(e_off, True)
    cd_x = _expand(jnp.broadcast_to(e_cd, (SUBLANE, LANE)), True)[0:1, :]

    xd = xs * _expand(dt, False)
    lane = lax.broadcasted_iota(jnp.int32, (q, D_SSD), 1)
    even_head = (lane // SSD_HEAD_DIM) % 2 == 0
    xd_even = jnp.where(even_head, xd, 0.0).astype(BF16)
    xd_odd = jnp.where(even_head, 0.0, xd).astype(BF16)
    xdte = (xs * _expand(dt * e_dte, False)).astype(BF16)
    cb16 = cm.astype(BF16)

    gw = SSD_HEADS // SSD_GROUPS * SSD_HEAD_DIM
    ys = []
    for g in range(SSD_GROUPS):
        cg = cb16[:, g * SSD_STATE:(g + 1) * SSD_STATE]
        bg_t = bm[:, g * SSD_STATE:(g + 1) * SSD_STATE].T.astype(BF16)
        cbm = _dot(cg, bg_t)
        ht_g = ht_ref[d, :, g * gw:(g + 1) * gw]
        y_g = _dot(cg, ht_g.astype(BF16)) * eoff_x[:, g * gw:(g + 1) * gw]
        halves = []
        for pr in range(2):
            acc = None
            for r in range(2):
                h = g * 4 + pr * 2 + r
                c = d * SSD_HEADS + h
                seg = cum[:, c:c + 1] - cum_t[c:c + 1, :]
                decay = jnp.exp(jnp.where(keep, seg, -jnp.inf))
                m = (cbm * decay).astype(BF16)
                src = xd_even if r == 0 else xd_odd
                blk = (h // 2) * LANE
                part = _dot(m, src[:, blk:blk + LANE])
                acc = part if acc is None else acc + part
            halves.append(acc)
        ys.append(y_g + jnp.concatenate(halves, axis=1))
        s_t = _dot(bg_t, xdte[:, g * gw:(g + 1) * gw])
        ht_ref[d, :, g * gw:(g + 1) * gw] = ht_g * cd_x[:, g * gw:(g + 1) * gw] + s_t
    return jnp.concatenate(ys, axis=1), xs


def _ssd_kernel(xf_ref, xb_ref, dtf_ref, dtb_ref, h0_ref, dtbias_ref, alog_ref, dskip_ref,
                yf_ref, yb_ref, *rest, nc, emit_state):
    if emit_state:
        st_ref, ht_ref = rest
    else:
        (ht_ref,) = rest
    k = pl.program_id(1)

    @pl.when(k == 0)
    def _():
        for d in range(2):
            ht_ref[d] = h0_ref[0, d].T

    a_row = -jnp.exp(alog_ref[...])
    dtbias = dtbias_ref[...]

    y, xs = _ssd_direction(xf_ref[...], dtf_ref[...], dtbias, a_row, ht_ref, 0, False)
    yf_ref[...] = y + xs * dskip_ref[...]

    y, _ = _ssd_direction(xb_ref[...], dtb_ref[...], dtbias, a_row, ht_ref, 1, True)
    yb_ref[...] = y

    if emit_state:
        @pl.when(k == nc - 1)
        def _():
            for d in range(2):
                st_ref[0, d] = ht_ref[d].T


def _ssd_scan(xbc, dt, h0t, dtbias, alog, dskip, nb, seq, emit_state):
    t = xbc.shape[0]
    q = SSD_CHUNK
    nc = seq // q

    def fwd(b, k):
        return b * nc + k

    def bwd(b, k):
        return b * nc + nc - 1 - k

    in_specs = [
        pl.BlockSpec((q, D_XBC), lambda b, k: (fwd(b, k), 0)),
        pl.BlockSpec((q, D_XBC), lambda b, k: (bwd(b, k), 0)),
        pl.BlockSpec((q, LANE), lambda b, k: (fwd(b, k), 0)),
        pl.BlockSpec((q, LANE), lambda b, k: (bwd(b, k), 0)),
        pl.BlockSpec((1, 2, D_SSD, SSD_STATE), lambda b, k: (b, 0, 0, 0)),
        _const_spec(dtbias.shape), _const_spec(alog.shape), _const_spec(dskip.shape)]
    out_specs = [pl.BlockSpec((q, D_SSD), lambda b, k: (fwd(b, k), 0)),
                 pl.BlockSpec((q, D_SSD), lambda b, k: (bwd(b, k), 0))]
    out_shape = [jax.ShapeDtypeStruct((t, D_SSD), F32), jax.ShapeDtypeStruct((t, D_SSD), F32)]
    if emit_state:
        out_specs.append(pl.BlockSpec((1, 2, D_SSD, SSD_STATE), lambda b, k: (b, 0, 0, 0)))
        out_shape.append(jax.ShapeDtypeStruct((nb, 2, D_SSD, SSD_STATE), F32))
    return pl.pallas_call(
        functools.partial(_ssd_kernel, nc=nc, emit_state=emit_state),
        grid=(nb, nc),
        in_specs=in_specs, out_specs=out_specs, out_shape=out_shape,
        scratch_shapes=[pltpu.VMEM((2, SSD_STATE, D_SSD), F32)],
        compiler_params=_cparams(("parallel", "arbitrary")),
        name="ssd_scan",
    )(xbc, xbc, dt, dt, h0t, dtbias, alog, dskip)


_CHALO = 16
_CPAD = CONF_KERNEL // 2
_ROWBLK = 2 * SUBLANE


def _mixout_kernel(x_ref, yf_ref, yb_ref, z_ref, glu_ref, glup_ref, glun_ref, mod_ref,
                   nrm_ref, cw_ref, cb_ref, lng_ref, lnb_ref, wo_ref, gffn_ref,
                   x1_ref, hf_ref, ext_ref, sh_ref, conv_ref, *, tm, seq_tiles):
    i = pl.program_id(0)
    has_prev = i % seq_tiles > 0
    has_next = i % seq_tiles < seq_tiles - 1

    def glu(ref):
        v = ref[...]
        return v[:, :CONF_CH] * jax.nn.sigmoid(v[:, CONF_CH:])

    n_ext = tm + 2 * _CHALO
    ext_ref[0:_CHALO, :] = jnp.where(has_prev, glu(glup_ref), 0.0)
    ext_ref[_CHALO:_CHALO + tm, :] = glu(glu_ref)
    ext_ref[_CHALO + tm:n_ext, :] = jnp.where(has_next, glu(glun_ref), 0.0)
    e = ext_ref[...]
    for r in range(SUBLANE):
        sh_ref[r, SUBLANE - r:SUBLANE - r + n_ext, :] = e

    def row_block(rb):
        base = rb * _ROWBLK
        nsub = _ROWBLK // SUBLANE
        accs = [jnp.broadcast_to(cb_ref[...], (SUBLANE, CONF_CH))] * nsub
        for j in range(CONF_KERNEL):
            off = _CHALO - _CPAD + j
            qq, r = divmod(off, SUBLANE)
            w8 = cw_ref[j * SUBLANE:(j + 1) * SUBLANE, :]
            accs = [a + w8 * sh_ref[r, pl.ds(base + (qq + 1 + k) * SUBLANE, SUBLANE), :]
                    for k, a in enumerate(accs)]
        for k, a in enumerate(accs):
            conv_ref[pl.ds(base + k * SUBLANE, SUBLANE), :] = a

    for rb in range(tm // _ROWBLK):
        row_block(rb)

    u = conv_ref[...]
    mu = jnp.mean(u, axis=-1, keepdims=True)
    uc = u - mu
    var = jnp.mean(uc * uc, axis=-1, keepdims=True)
    u = _silu(uc * lax.rsqrt(var + EPS) * lng_ref[...] + lnb_ref[...])

    y = (yf_ref[...] + yb_ref[...]) * _silu(z_ref[...])
    y = _rms(y) * nrm_ref[...]

    out = _dot(y.astype(BF16), wo_ref[0:D_SSD, :]) + _dot(u.astype(BF16), wo_ref[D_SSD:, :])
    m = mod_ref[0]
    x1 = x_ref[...] + m[2:3] * out
    x1_ref[...] = x1
    hf_ref[...] = _modnorm(x1, gffn_ref[...], m[4:5], m[3:4]).astype(BF16)


def _mix_out(x2, yf, yb, z, glu, mod, nrm, cw, cb, lng, lnb, wo, gffn, seq, per_batch, tm):
    t = x2.shape[0]
    seq_tiles = seq // tm
    hb = tm // _CHALO
    last_hb = t // _CHALO - 1
    row = lambda w: pl.BlockSpec((tm, w), lambda i: (i, 0))
    vec = _const_spec((1, D_MODEL))
    return pl.pallas_call(
        functools.partial(_mixout_kernel, tm=tm, seq_tiles=seq_tiles),
        grid=(t // tm,),
        in_specs=[row(D_MODEL), row(D_SSD), row(D_SSD), row(D_SSD), row(2 * CONF_CH),
                  pl.BlockSpec((_CHALO, 2 * CONF_CH), lambda i: (jnp.maximum(i * hb - 1, 0), 0)),
                  pl.BlockSpec((_CHALO, 2 * CONF_CH), lambda i: (jnp.minimum((i + 1) * hb, last_hb), 0)),
                  _mod_spec(seq_tiles, per_batch),
                  vec, _const_spec(cw.shape), vec, vec, vec, _const_spec(wo.shape), vec],
        out_specs=[row(D_MODEL), row(D_MODEL)],
        out_shape=[jax.ShapeDtypeStruct((t, D_MODEL), F32), jax.ShapeDtypeStruct((t, D_MODEL), BF16)],
        scratch_shapes=[pltpu.VMEM((tm + 2 * _CHALO, CONF_CH), F32),
                        pltpu.VMEM((SUBLANE, tm + 2 * _CHALO + SUBLANE, CONF_CH), F32),
                        pltpu.VMEM((tm, CONF_CH), F32)],
        compiler_params=_cparams(("parallel",)),
        name="mix_out",
    )(x2, yf, yb, z, glu, glu, glu, mod, nrm, cw, cb, lng, lnb, wo, gffn)


_FF_CHUNK = 256
_FF_NCH = D_FF // _FF_CHUNK
_FHALO = 128
_FF_ROWBLK = 32
_FF_PAD = SUBLANE
_FFN_FLAGS = None


def _ffn_kernel(*refs, tm, seq_tiles, period, row_taps, halo, final_norm):
    refs = list(refs)
    hf_ref = refs.pop(0)
    if halo:
        hfp_ref, hfn_ref = refs.pop(0), refs.pop(0)
    x1_ref, mod_ref, wup_ref, cw_ref, wdn_ref = [refs.pop(0) for _ in range(5)]
    if final_norm:
        nf_ref, o_ref = refs.pop(0), refs.pop(0)
    else:
        modn_ref, gn_ref, o_ref, hn_ref = [refs.pop(0) for _ in range(4)]
    ext_ref = refs.pop(0) if halo else hf_ref
    sl_ref, sc_ref, sr_ref, ml_ref, mr_ref, g_ref, acc_ref = refs
    i = pl.program_id(0)
    pad = _FF_PAD
    n_ext = tm + 2 * halo

    if halo:
        has_prev = i % seq_tiles > 0
        has_next = i % seq_tiles < seq_tiles - 1
        ext_ref[0:halo, :] = jnp.where(has_prev, hfp_ref[_FHALO - halo:_FHALO, :], 0.0).astype(BF16)
        ext_ref[halo:halo + tm, :] = hf_ref[...]
        ext_ref[halo + tm:n_ext, :] = jnp.where(has_next, hfn_ref[0:halo, :], 0.0).astype(BF16)

    col = (lax.broadcasted_iota(jnp.int32, (n_ext, _FF_CHUNK), 0) + (period - halo % period)) % period
    ml_ref[...] = jnp.where(col == period - 1, 0.0, 1.0)
    mr_ref[...] = jnp.where(col == 0, 0.0, 1.0)
    zeros_pad = jnp.zeros((2 * pad, _FF_CHUNK), F32)
    for ref in (sl_ref, sr_ref):
        for k in range(4):
            ref[k, 0:2 * pad, :] = zeros_pad
            ref[k, n_ext:n_ext + 2 * pad, :] = zeros_pad
    acc_ref[...] = jnp.zeros((tm, D_MODEL), F32)

    def up(c, slot):
        for half in range(2):
            c0 = half * D_FF + c * _FF_CHUNK
            u = _dot(ext_ref[...], wup_ref[:, c0:c0 + _FF_CHUNK])
            k = slot * 2 + half
            sc_ref[k, pad:pad + n_ext, :] = u
            sl_ref[k, pad + 1:pad + 1 + n_ext, :] = u * ml_ref[...]
            sr_ref[k, pad - 1:pad - 1 + n_ext, :] = u * mr_ref[...]

    def conv(c, slot):
        w = [cw_ref[:, half * D_FF + c * _FF_CHUNK:half * D_FF + (c + 1) * _FF_CHUNK] for half in range(2)]
        for rb in range(tm // _FF_ROWBLK):
            r0 = rb * _FF_ROWBLK
            outs = []
            for half in range(2):
                k = slot * 2 + half
                acc = None
                for dr in row_taps:
                    base = pad + halo + dr * period + r0
                    for dc, ref in enumerate((sl_ref, sc_ref, sr_ref)):
                        tap = (dr + 1) * 3 + dc
                        term = w[half][tap:tap + 1, :] * ref[k, pl.ds(base, _FF_ROWBLK), :]
                        acc = term if acc is None else acc + term
                outs.append(acc)
            g_ref[slot, r0:r0 + _FF_ROWBLK, :] = (outs[0] * _silu(outs[1])).astype(BF16)

    def down(c, slot):
        acc_ref[...] += _dot(g_ref[slot], wdn_ref[c * _FF_CHUNK:(c + 1) * _FF_CHUNK, :])

    last = _FF_NCH - 1
    up(0, 0)
    up(1, 1)
    conv(0, 0)

    for c in range(1, last + 1):
        if c < last:
            up(c + 1, (c + 1) % 2)
        conv(c, c % 2)
        down(c - 1, (c - 1) % 2)
    down(last, last % 2)

    m = mod_ref[0]
    x2 = x1_ref[...] + m[5:6] * acc_ref[...]
    if final_norm:
        o_ref[...] = _rms(x2) * nf_ref[...]
    else:
        o_ref[...] = x2
        mn = modn_ref[0]
        hn_ref[...] = _modnorm(x2, gn_ref[...], mn[1:2], mn[0:1])


def _conv_ffn(hf, x1, mod, wup, cw, wdn, tail, seq, per_batch, tm, latent, final_norm):
    t = hf.shape[0]
    seq_tiles = seq // tm
    if latent:
        period, row_taps, halo = GRID_W, (-1, 0, 1), GRID_W + 16
    else:
        period, row_taps, halo = seq, (0,), 0
    n_ext = tm + 2 * halo
    row = lambda w: pl.BlockSpec((tm, w), lambda i: (i, 0))
    in_specs = [row(D_MODEL)]
    args = [hf]
    scratch = []
    if halo:
        hb = tm // _FHALO
        last_hb = t // _FHALO - 1
        in_specs += [pl.BlockSpec((_FHALO, D_MODEL), lambda i: (jnp.maximum(i * hb - 1, 0), 0)),
                     pl.BlockSpec((_FHALO, D_MODEL), lambda i: (jnp.minimum((i + 1) * hb, last_hb), 0))]
        args += [hf, hf]
        scratch.append(pltpu.VMEM((n_ext, D_MODEL), BF16))
    in_specs += [row(D_MODEL), _mod_spec(seq_tiles, per_batch), _const_spec(wup.shape, single=True),
                 _const_spec(cw.shape), _const_spec(wdn.shape, single=True)]
    args += [x1, mod, wup, cw, wdn]
    vec = _const_spec((1, D_MODEL))
    if final_norm:
        in_specs += [vec]
        out_specs = row(D_MODEL)
        out_shape = jax.ShapeDtypeStruct((t, D_MODEL), F32)
    else:
        in_specs += [_mod_spec(seq_tiles, per_batch), vec]
        out_specs = [row(D_MODEL), row(D_MODEL)]
        out_shape = [jax.ShapeDtypeStruct((t, D_MODEL), F32)] * 2
    args += list(tail)
    ubuf = pltpu.VMEM((4, n_ext + 2 * _FF_PAD, _FF_CHUNK), F32)
    mask = pltpu.VMEM((n_ext, _FF_CHUNK), F32)
    scratch += [ubuf, ubuf, ubuf, mask, mask,
                pltpu.VMEM((2, tm, _FF_CHUNK), BF16), pltpu.VMEM((tm, D_MODEL), F32)]
    return pl.pallas_call(
        functools.partial(_ffn_kernel, tm=tm, seq_tiles=seq_tiles, period=period,
                          row_taps=row_taps, halo=halo, final_norm=final_norm),
        grid=(t // tm,),
        in_specs=in_specs,
        out_specs=out_specs,
        out_shape=out_shape,
        scratch_shapes=scratch,
        compiler_params=_cparams(("parallel",), _FFN_FLAGS),
        name="conv_ffn",
    )(*args)


def _dft_tables(n):
    k = np.arange(n, dtype=np.int64)
    ang = 2.0 * np.pi * ((k[:, None] * k[None, :]) % n) / n
    s = 1.0 / math.sqrt(n)
    return np.cos(ang) * s, np.sin(ang) * s


def _seqdft_direct_kernel(h_ref, m_ref, pr_ref, pi_ref, *, seq):
    p = _dot(m_ref[...], h_ref[...].astype(BF16))
    pr_ref[...] = p[0:seq].astype(BF16)
    pi_ref[...] = p[seq:2 * seq].astype(BF16)


def _seq_dft_direct(h2, nb, seq):
    c, s = _dft_tables(seq)
    mat = jnp.asarray(np.concatenate([c, -s], axis=0), BF16)
    t = h2.shape[0]
    tc = 512
    blk = pl.BlockSpec((seq, tc), lambda b, j: (b, j))
    return pl.pallas_call(
        functools.partial(_seqdft_direct_kernel, seq=seq),
        grid=(nb, D_MODEL // tc),
        in_specs=[blk, _const_spec(mat.shape)],
        out_specs=[blk, blk],
        out_shape=[jax.ShapeDtypeStruct((t, D_MODEL), BF16)] * 2,
        compiler_params=_cparams(("parallel", "parallel")), name="seq_dft_direct",
    )(h2, mat)


_DFT_PITCH = GRID_W + 4
_DFT_TC = 2 * LANE


def _seqdft_2stage_kernel(h_ref, m1_ref, m2_ref, pr_ref, pi_ref, xa_ref, yr_ref, yi_ref, pb_ref, *, w):
    nlb = _DFT_TC // LANE
    pitch = _DFT_PITCH
    lanes = lambda v, j: v[:, j * LANE:(j + 1) * LANE]

    def gather(ref, r):
        return jnp.concatenate([ref[j, pl.ds(r, w, stride=pitch), :] for j in range(nlb)], axis=1)

    for n1 in range(w):
        rows = h_ref[n1 * w:(n1 + 1) * w, :]
        for j in range(nlb):
            xa_ref[j, n1 * pitch:n1 * pitch + w, :] = lanes(rows, j)
    for n2 in range(w):
        y = _dot(m1_ref[n2], gather(xa_ref, n2).astype(BF16))
        for j in range(nlb):
            yr_ref[j, n2 * pitch:n2 * pitch + w, :] = lanes(y[0:w], j)
            yi_ref[j, n2 * pitch:n2 * pitch + w, :] = lanes(y[w:2 * w], j)
    m2 = m2_ref[...]
    for k1 in range(w):
        yk = jnp.concatenate([gather(yr_ref, k1), gather(yi_ref, k1)], axis=0).astype(BF16)
        p = _dot(m2, yk)
        for j in range(nlb):
            xa_ref[j, pl.ds(k1, w, stride=pitch), :] = lanes(p[0:w], j)
            pb_ref[j, pl.ds(k1, w, stride=pitch), :] = lanes(p[w:2 * w], j)
    for k2 in range(w):
        for ref, out in ((xa_ref, pr_ref), (pb_ref, pi_ref)):
            out[k2 * w:(k2 + 1) * w, :] = jnp.concatenate(
                [ref[j, k2 * pitch:k2 * pitch + w, :] for j in range(nlb)], axis=1).astype(BF16)


def _seq_dft_2stage(h2, nb, seq):
    w = GRID_W
    assert seq == w * w
    k1 = np.arange(w, dtype=np.int64)[:, None]
    n1 = np.arange(w, dtype=np.int64)[None, :]
    m1 = np.zeros((w, 2 * w, w), np.float64)
    for n2 in range(w):
        ang = 2.0 * np.pi * ((k1 * (w * n1 + n2)) % seq) / seq
        m1[n2, :w] = np.cos(ang)
        m1[n2, w:] = -np.sin(ang)
    m1 /= math.sqrt(seq)
    ang2 = 2.0 * np.pi * ((k1 * n1) % w) / w
    c2, s2 = np.cos(ang2), np.sin(ang2)
    m2 = np.block([[c2, s2], [-s2, c2]])
    t = h2.shape[0]
    tc = _DFT_TC
    blk = pl.BlockSpec((seq, tc), lambda b, j: (b, j))
    pitched = pltpu.VMEM((tc // LANE, w * _DFT_PITCH, LANE), F32)
    return pl.pallas_call(
        functools.partial(_seqdft_2stage_kernel, w=w),
        grid=(nb, D_MODEL // tc),
        in_specs=[blk, _const_spec(m1.shape), _const_spec(m2.shape)],
        out_specs=[blk, blk],
        out_shape=[jax.ShapeDtypeStruct((t, D_MODEL), BF16)] * 2,
        scratch_shapes=[pitched] * 4,
        compiler_params=_cparams(("parallel", "parallel")), name="seq_dft_2stage",
    )(h2, jnp.asarray(m1, BF16), jnp.asarray(m2, BF16))


def _fnetw_kernel(dft_ref, w_ref, o_ref):
    o_ref[...] = _dot(dft_ref[...], w_ref[...].astype(BF16)).astype(BF16)


def _fnet_weights(fnet_w):
    gsz = D_MODEL // FNET_GROUPS
    c, s = _dft_tables(gsz)
    eye = np.eye(FNET_GROUPS)
    dft = jnp.asarray(np.concatenate([np.kron(eye, c), np.kron(eye, s)], axis=0), BF16)
    tmw = 512
    return pl.pallas_call(
        _fnetw_kernel, grid=(2 * D_MODEL // tmw,),
        in_specs=[pl.BlockSpec((tmw, D_MODEL), lambda i: (i, 0)), _const_spec((D_MODEL, D_MODEL))],
        out_specs=pl.BlockSpec((tmw, D_MODEL), lambda i: (i, 0)),
        out_shape=jax.ShapeDtypeStruct((2 * D_MODEL, D_MODEL), BF16),
        compiler_params=_cparams(("parallel",)), name="fnet_weights",
    )(dft, fnet_w)


def _fnetout_kernel(pr_ref, pi_ref, x_ref, mod_ref, w_ref, b_ref, gffn_ref, x1_ref, hf_ref):
    out = _dot(pr_ref[...], w_ref[0:D_MODEL, :]) + _dot(pi_ref[...], w_ref[D_MODEL:, :]) + b_ref[...]
    m = mod_ref[0]
    x1 = x_ref[...] + m[2:3] * out
    x1_ref[...] = x1
    hf_ref[...] = _modnorm(x1, gffn_ref[...], m[4:5], m[3:4]).astype(BF16)


def _fnet_out(pr, pi, x2, mod, wcs, b, gffn, seq, per_batch, tm):
    t = x2.shape[0]
    row = pl.BlockSpec((tm, D_MODEL), lambda i: (i, 0))
    vec = _const_spec((1, D_MODEL))
    return pl.pallas_call(
        _fnetout_kernel, grid=(t // tm,),
        in_specs=[row, row, row, _mod_spec(seq // tm, per_batch), _const_spec(wcs.shape), vec, vec],
        out_specs=[row, row],
        out_shape=[jax.ShapeDtypeStruct((t, D_MODEL), F32), jax.ShapeDtypeStruct((t, D_MODEL), BF16)],
        compiler_params=_cparams(("parallel",)), name="fnet_out",
    )(pr, pi, x2, mod, wcs, b, gffn)


def _trunk(x, mods, h0t, per_batch, latent, p):
    nb, seq, _ = x.shape
    t = nb * seq
    x2 = x.reshape(t, D_MODEL)
    tm = 256

    z, xbc, glu, dt = _in_proj(x2, mods[0], p['g_mix'][0], p['wz'], p['wx'], p['wg'], p['wd'],
                               p['ssd_cw'], p['ssd_cb'], seq, per_batch, tm)
    scan = _ssd_scan(xbc, dt, h0t, p['dtbias'], p['alog'], p['dskip'], nb, seq, emit_state=not latent)
    yf, yb = scan[0], scan[1]
    state = scan[2] if not latent else None
    x1, hf = _mix_out(x2, yf, yb, z, glu, mods[0], p['ssd_norm'], p['conf_cw'], p['conf_cb'],
                      p['ln_g'], p['ln_b'], p['w_out'], p['g_ffn'][0], seq, per_batch, tm)
    ftm = 512
    x2, hm = _conv_ffn(hf, x1, mods[0], p['w_up'][0], p['ffn_cw'][0], p['w_down'][0],
                       (mods[1], p['g_mix'][1]), seq, per_batch, ftm, latent, final_norm=False)

    if seq == GRID_W * GRID_W:
        pr, pi = _seq_dft_2stage(hm, nb, seq)
    else:
        pr, pi = _seq_dft_direct(hm, nb, seq)
    x1, hf = _fnet_out(pr, pi, x2, mods[1], p['fnet_wcs'], p['fnet_b'], p['g_ffn'][1],
                       seq, per_batch, tm if seq == tm else ftm)
    y = _conv_ffn(hf, x1, mods[1], p['w_up'][1], p['ffn_cw'][1], p['w_down'][1], (p['norm_f'],),
                  seq, per_batch, ftm, latent, final_norm=True)
    return y.reshape(nb, seq, D_MODEL), state


def kernel(x_prompt, x_sample, c, state_ssd, c_ctx, w_mod, b_mod, g_mix, g_ffn, w_in, ssd_conv_w, ssd_conv_b, ssd_dt_bias, ssd_a_log, ssd_d, ssd_norm, conf_conv_w, conf_conv_b, conf_ln_g, conf_ln_b, w_out, fnet_w, fnet_b, ffn_w_up, ffn_conv_w, ffn_w_down, norm_f):
    depth = w_mod.shape[0]
    nb_ctx = x_prompt.shape[0]
    nb_lat = x_sample.shape[0]
    assert nb_lat <= CTX_ROW

    cond = jnp.zeros((MOD_ROWS, D_MODEL), F32).at[:nb_lat].set(c).at[CTX_ROW].set(c_ctx)
    mod = _modulation(cond, w_mod, b_mod).reshape(depth, MOD_ROWS, N_MOD, D_MODEL)
    mods = [mod[i] for i in range(depth)]

    def pad_lanes(v, fill=0.0):
        return jnp.pad(v, ((0, 0), (0, LANE - v.shape[1])), constant_values=fill)

    w_in0 = w_in[0].astype(BF16)
    o_dt = D_SSD + D_XBC
    o_glu = o_dt + 2 * SSD_HEADS
    vec = lambda v: v.reshape(1, -1)
    p = dict(
        g_mix=[vec(g_mix[i]) for i in range(depth)],
        g_ffn=[vec(g_ffn[i]) for i in range(depth)],
        wz=w_in0[:, :D_SSD], wx=w_in0[:, D_SSD:o_dt], wd=pad_lanes(w_in0[:, o_dt:o_glu]), wg=w_in0[:, o_glu:],
        ssd_cw=jnp.repeat(ssd_conv_w[0], SUBLANE, axis=0),
        ssd_cb=vec(ssd_conv_b[0]),
        dtbias=pad_lanes(ssd_dt_bias[0].reshape(1, -1)),
        alog=pad_lanes(ssd_a_log[0].reshape(1, -1)),
        dskip=vec(jnp.repeat(ssd_d[0], SSD_HEAD_DIM)),
        ssd_norm=vec(ssd_norm[0]),
        conf_cw=jnp.repeat(conf_conv_w[0], SUBLANE, axis=0),
        conf_cb=vec(conf_conv_b[0]), ln_g=vec(conf_ln_g[0]), ln_b=vec(conf_ln_b[0]),
        w_out=w_out[0].astype(BF16),
        fnet_wcs=_fnet_weights(fnet_w[0]), fnet_b=vec(fnet_b[0]),
        w_up=[ffn_w_up[i].astype(BF16) for i in range(depth)],
        ffn_cw=[jnp.pad(ffn_conv_w[i].reshape(9, -1), ((0, 7), (0, 0))) for i in range(depth)],
        w_down=[ffn_w_down[i].astype(BF16) for i in range(depth)],
        norm_f=vec(norm_f),
    )

    h0_ctx = jnp.zeros((nb_ctx, 2, D_SSD, SSD_STATE), F32)
    y_prompt, st = _trunk(x_prompt, mods, h0_ctx, False, False, p)
    new_state = st.reshape(nb_ctx, 1, 2, SSD_HEADS, SSD_HEAD_DIM, SSD_STATE).astype(x_prompt.dtype)

    h0_lat = state_ssd[:, 0].astype(F32).reshape(nb_lat, 2, D_SSD, SSD_STATE)
    y_sample, _ = _trunk(x_sample, mods, h0_lat, True, True, p)
    return (y_prompt, y_sample, new_state)
```

```python
import functools
import math

import numpy as np
import jax
import jax.numpy as jnp
from jax import lax
from jax.experimental import pallas as pl
from jax.experimental.pallas import tpu as pltpu

F32 = jnp.float32
BF16 = jnp.bfloat16

D_MODEL = 1024
GRID_W = 64
SSD_HEAD_DIM = 64
SSD_HEADS = 16
D_SSD = SSD_HEADS * SSD_HEAD_DIM
SSD_STATE = 128
SSD_GROUPS = 4
SSD_CONV = 5
SSD_CHUNK = 256
CONF_CH = D_MODEL
CONF_KERNEL = 31
FNET_GROUPS = 8
D_FF = 2816
N_MOD = 6
EPS = 1e-6
D_XBC = D_SSD + 2 * SSD_GROUPS * SSD_STATE

LANE = 128
SUBLANE = 8
VMEM_LIMIT = 56 * 1024 * 1024

MOD_ROWS = 16
CTX_ROW = 8


def _dot(a, b):
    return jnp.dot(a, b, preferred_element_type=F32)


def _silu(v):
    return v * jax.nn.sigmoid(v)


def _split2(v):
    hi = v.astype(BF16)
    lo = (v - hi.astype(F32)).astype(BF16)
    return hi, lo


def _split3(v):
    hi = v.astype(BF16)
    r = v - hi.astype(F32)
    mid = r.astype(BF16)
    lo = (r - mid.astype(F32)).astype(BF16)
    return hi, mid, lo


def _rms(x):
    return x * lax.rsqrt(jnp.mean(x * x, axis=-1, keepdims=True) + EPS)


def _modnorm(x, g, scale, shift):
    return (_rms(x) * g) * (1.0 + scale) + shift


def _cparams(sem, flags=None):
    return pltpu.CompilerParams(dimension_semantics=sem, vmem_limit_bytes=VMEM_LIMIT, flags=flags)


def _const_spec(shape, single=False):
    nd = len(shape)
    if single:
        return pl.BlockSpec(shape, lambda *_: (0,) * nd, pipeline_mode=pl.Buffered(1))
    return pl.BlockSpec(shape, lambda *_: (0,) * nd)


def _mod_kernel(cond_ref, w_ref, b_ref, o_ref):
    s = _silu(cond_ref[...])
    s_hi, s_lo = _split2(s)
    w_hi, w_lo = _split2(w_ref[0])
    acc = _dot(s_hi, w_hi) + _dot(s_hi, w_lo) + _dot(s_lo, w_hi)
    o_ref[0] = acc + b_ref[0]


def _modulation(cond, w_mod, b_mod):
    depth = w_mod.shape[0]
    n = w_mod.shape[2]
    tn = D_MODEL
    return pl.pallas_call(
        _mod_kernel,
        grid=(depth, n // tn),
        in_specs=[
            _const_spec((MOD_ROWS, D_MODEL)),
            pl.BlockSpec((1, D_MODEL, tn), lambda i, j: (i, 0, j)),
            pl.BlockSpec((1, 1, tn), lambda i, j: (i, 0, j)),
        ],
        out_specs=pl.BlockSpec((1, MOD_ROWS, tn), lambda i, j: (i, 0, j)),
        out_shape=jax.ShapeDtypeStruct((depth, MOD_ROWS, n), F32),
        compiler_params=_cparams(("parallel", "parallel")),
        name="modulation",
    )(cond, w_mod, b_mod.reshape(depth, 1, n))


def _mod_spec(seq_tiles, per_batch):
    if per_batch:
        return pl.BlockSpec((1, N_MOD, D_MODEL), lambda i: (i // seq_tiles, 0, 0))
    return pl.BlockSpec((1, N_MOD, D_MODEL), lambda i: (CTX_ROW, 0, 0))


_XHALO = 16
_XPAD = SUBLANE
_XSHIFTS = tuple(d for d in range(-(SSD_CONV // 2), SSD_CONV // 2 + 1) if d != 0)


def _inproj_kernel(x_ref, xp_ref, xn_ref, mod_ref, g_ref, wz_ref, wx_ref, wg_ref, wd_ref, cw_ref, cb_ref,
                   z_ref, xbc_ref, glu_ref, dt_ref, xe_ref, sh_ref, *, tm, seq_tiles):
    i = pl.program_id(0)
    has_prev = i % seq_tiles > 0
    has_next = i % seq_tiles < seq_tiles - 1
    m = mod_ref[0]
    g = g_ref[...]
    norm = lambda v: _modnorm(v, g, m[1:2], m[0:1]).astype(BF16)
    h = norm(x_ref[...])
    n_ext = tm + 2 * _XHALO

    hext = jnp.concatenate([norm(xp_ref[...]), h, norm(xn_ref[...])], axis=0)
    xe = _dot(hext, wx_ref[...])
    row = lax.broadcasted_iota(jnp.int32, (n_ext, 1), 0)
    inside = ((row >= _XHALO) | has_prev) & ((row < _XHALO + tm) | has_next)
    xe = jnp.where(inside, xe, 0.0)
    xe_ref[...] = xe
    for k, d in enumerate(_XSHIFTS):
        sh_ref[k, _XPAD - d:_XPAD - d + n_ext, :] = xe

    z_ref[...] = _dot(h, wz_ref[...])
    glu_ref[...] = _dot(h, wg_ref[...])
    dt_ref[...] = _dot(h, wd_ref[...])

    for rb in range(tm // (2 * SUBLANE)):
        base = rb * 2 * SUBLANE
        accs = [jnp.broadcast_to(cb_ref[...], (SUBLANE, D_XBC))] * 2
        for j in range(SSD_CONV):
            d = j - SSD_CONV // 2
            w8 = cw_ref[j * SUBLANE:(j + 1) * SUBLANE, :]
            for q in range(2):
                r0 = base + q * SUBLANE + _XHALO
                if d == 0:
                    v = xe_ref[pl.ds(r0, SUBLANE), :]
                else:
                    v = sh_ref[_XSHIFTS.index(d), pl.ds(r0 + _XPAD, SUBLANE), :]
                accs[q] = accs[q] + w8 * v
        for q in range(2):
            xbc_ref[pl.ds(base + q * SUBLANE, SUBLANE), :] = _silu(accs[q])


def _in_proj(x2, mod, g, wz, wx, wg, wd, cw, cb, seq, per_batch, tm):
    t = x2.shape[0]
    seq_tiles = seq // tm
    hb = tm // _XHALO
    last_hb = t // _XHALO - 1
    row = lambda w: pl.BlockSpec((tm, w), lambda i: (i, 0))
    n_ext = tm + 2 * _XHALO
    return pl.pallas_call(
        functools.partial(_inproj_kernel, tm=tm, seq_tiles=seq_tiles),
        grid=(t // tm,),
        in_specs=[row(D_MODEL),
                  pl.BlockSpec((_XHALO, D_MODEL), lambda i: (jnp.maximum(i * hb - 1, 0), 0)),
                  pl.BlockSpec((_XHALO, D_MODEL), lambda i: (jnp.minimum((i + 1) * hb, last_hb), 0)),
                  _mod_spec(seq_tiles, per_batch), _const_spec((1, D_MODEL)),
                  _const_spec(wz.shape), _const_spec(wx.shape), _const_spec(wg.shape),
                  _const_spec(wd.shape), _const_spec(cw.shape), _const_spec(cb.shape)],
        out_specs=[row(D_SSD), row(D_XBC), row(2 * CONF_CH), row(LANE)],
        out_shape=[jax.ShapeDtypeStruct((t, D_SSD), F32), jax.ShapeDtypeStruct((t, D_XBC), F32),
                   jax.ShapeDtypeStruct((t, 2 * CONF_CH), F32), jax.ShapeDtypeStruct((t, LANE), F32)],
        scratch_shapes=[pltpu.VMEM((n_ext, D_XBC), F32),
                        pltpu.VMEM((len(_XSHIFTS), n_ext + 2 * _XPAD, D_XBC), F32)],
        compiler_params=_cparams(("parallel",)),
        name="in_proj",
    )(x2, x2, x2, mod, g, wz, wx, wg, wd, cw, cb)


def _ssd_direction(xbc, dt_raw, dtb, a_row, ht_ref, d, rev):
    q = SSD_CHUNK
    xs = xbc[:, :D_SSD]
    bm = xbc[:, D_SSD:D_SSD + SSD_GROUPS * SSD_STATE]
    cm = xbc[:, D_SSD + SSD_GROUPS * SSD_STATE:]
    dt = jax.nn.softplus(dt_raw + dtb)
    la = dt * a_row

    ri = lax.broadcasted_iota(jnp.int32, (q, q), 0)
    ci = lax.broadcasted_iota(jnp.int32, (q, q), 1)
    keep = (ci >= ri) if rev else (ci <= ri)
    tri = jnp.where(keep, 1.0, 0.0).astype(BF16)
    tri_t = jnp.where((ri >= ci) if rev else (ri <= ci), 1.0, 0.0).astype(BF16)
    cum = sum(_dot(tri, p) for p in _split3(la))
    cum_t = sum(_dot(p, tri_t) for p in _split3(la.T))

    end = 0 if rev else q - 1
    cum_end = cum[end:end + 1, :]
    e_off = jnp.exp(cum)
    e_dte = jnp.exp(cum_end - cum)
    e_cd = jnp.exp(cum_end)

    er = lax.broadcasted_iota(jnp.int32, (LANE, D_SSD), 0)
    ec = lax.broadcasted_iota(jnp.int32, (LANE, D_SSD), 1)
    expand = jnp.where(er == ec // SSD_HEAD_DIM + d * SSD_HEADS, 1.0, 0.0).astype(BF16)

    def _expand(v, exact):
        if not exact:
            return _dot(v.astype(BF16), expand)
        hi, lo = _split2(v)
        return _dot(hi, expand) + _dot(lo, expand)

    eoff_x = _expand(e_off, False)
    cd_x = _expand(jnp.broadcast_to(e_cd, (SUBLANE, LANE)), True)[0:1, :]

    xd = xs * _expand(dt, False)
    lane = lax.broadcasted_iota(jnp.int32, (q, D_SSD), 1)
    even_head = (lane // SSD_HEAD_DIM) % 2 == 0
    xd_even = jnp.where(even_head, xd, 0.0).astype(BF16)
    xd_odd = jnp.where(even_head, 0.0, xd).astype(BF16)
    xdte = (xs * _expand(dt * e_dte, False)).astype(BF16)
    cb16 = cm.astype(BF16)

    gw = SSD_HEADS // SSD_GROUPS * SSD_HEAD_DIM
    ys = []
    for g in range(SSD_GROUPS):
        cg = cb16[:, g * SSD_STATE:(g + 1) * SSD_STATE]
        bg_t = bm[:, g * SSD_STATE:(g + 1) * SSD_STATE].T.astype(BF16)
        cbm = _dot(cg, bg_t)
        ht_g = ht_ref[d, :, g * gw:(g + 1) * gw]
        y_g = _dot(cg, ht_g.astype(BF16)) * eoff_x[:, g * gw:(g + 1) * gw]
        halves = []
        for pr in range(2):
            acc = None
            for r in range(2):
                h = g * 4 + pr * 2 + r
                c = d * SSD_HEADS + h
                seg = cum[:, c:c + 1] - cum_t[c:c + 1, :]
                decay = jnp.exp(jnp.where(keep, seg, -jnp.inf))
                m = (cbm * decay).astype(BF16)
                src = xd_even if r == 0 else xd_odd
                blk = (h // 2) * LANE
                part = _dot(m, src[:, blk:blk + LANE])
                acc = part if acc is None else acc + part
            halves.append(acc)
        ys.append(y_g + jnp.concatenate(halves, axis=1))
        s_t = _dot(bg_t, xdte[:, g * gw:(g + 1) * gw])
        ht_ref[d, :, g * gw:(g + 1) * gw] = ht_g * cd_x[:, g * gw:(g + 1) * gw] + s_t
    return jnp.concatenate(ys, axis=1), xs


def _ssd_kernel(xf_ref, xb_ref, dtf_ref, dtb_ref, h0_ref, dtbias_ref, alog_ref, dskip_ref,
                yf_ref, yb_ref, *rest, nc, emit_state):
    if emit_state:
        st_ref, ht_ref = rest
    else:
        (ht_ref,) = rest
    k = pl.program_id(1)

    @pl.when(k == 0)
    def _():
        for d in range(2):
            ht_ref[d] = h0_ref[0, d].T

    a_row = -jnp.exp(alog_ref[...])
    dtbias = dtbias_ref[...]

    y, xs = _ssd_direction(xf_ref[...], dtf_ref[...], dtbias, a_row, ht_ref, 0, False)
    yf_ref[...] = y + xs * dskip_ref[...]

    y, _ = _ssd_direction(xb_ref[...], dtb_ref[...], dtbias, a_row, ht_ref, 1, True)
    yb_ref[...] = y

    if emit_state:
        @pl.when(k == nc - 1)
        def _():
            for d in range(2):
                st_ref[0, d] = ht_ref[d].T


def _ssd_scan(xbc, dt, h0t, dtbias, alog, dskip, nb, seq, emit_state):
    t = xbc.shape[0]
    q = SSD_CHUNK
    nc = seq // q

    def fwd(b, k):
        return b * nc + k

    def bwd(b, k):
        return b * nc + nc - 1 - k

    in_specs = [
        pl.BlockSpec((q, D_XBC), lambda b, k: (fwd(b, k), 0)),
        pl.BlockSpec((q, D_XBC), lambda b, k: (bwd(b, k), 0)),
        pl.BlockSpec((q, LANE), lambda b, k: (fwd(b, k), 0)),
        pl.BlockSpec((q, LANE), lambda b, k: (bwd(b, k), 0)),
        pl.BlockSpec((1, 2, D_SSD, SSD_STATE), lambda b, k: (b, 0, 0, 0)),
        _const_spec(dtbias.shape), _const_spec(alog.shape), _const_spec(dskip.shape)]
    out_specs = [pl.BlockSpec((q, D_SSD), lambda b, k: (fwd(b, k), 0)),
                 pl.BlockSpec((q, D_SSD), lambda b, k: (bwd(b, k), 0))]
    out_shape = [jax.ShapeDtypeStruct((t, D_SSD), F32), jax.ShapeDtypeStruct((t, D_SSD), F32)]
    if emit_state:
        out_specs.append(pl.BlockSpec((1, 2, D_SSD, SSD_STATE), lambda b, k: (b, 0, 0, 0)))
        out_shape.append(jax.ShapeDtypeStruct((nb, 2, D_SSD, SSD_STATE), F32))
    return pl.pallas_call(
        functools.partial(_ssd_kernel, nc=nc, emit_state=emit_state),
        grid=(nb, nc),
        in_specs=in_specs, out_specs=out_specs, out_shape=out_shape,
        scratch_shapes=[pltpu.VMEM((2, SSD_STATE, D_SSD), F32)],
        compiler_params=_cparams(("parallel", "arbitrary")),
        name="ssd_scan",
    )(xbc, xbc, dt, dt, h0t, dtbias, alog, dskip)


_CHALO = 16
_CPAD = CONF_KERNEL // 2
_ROWBLK = 2 * SUBLANE


def _mixout_kernel(x_ref, yf_ref, yb_ref, z_ref, glu_ref, glup_ref, glun_ref, mod_ref,
                   nrm_ref, cw_ref, cb_ref, lng_ref, lnb_ref, wo_ref, gffn_ref,
                   x1_ref, hf_ref, ext_ref, sh_ref, conv_ref, *, tm, seq_tiles):
    i = pl.program_id(0)
    has_prev = i % seq_tiles > 0
    has_next = i % seq_tiles < seq_tiles - 1

    def glu(ref):
        v = ref[...]
        return v[:, :CONF_CH] * jax.nn.sigmoid(v[:, CONF_CH:])

    n_ext = tm + 2 * _CHALO
    ext_ref[0:_CHALO, :] = jnp.where(has_prev, glu(glup_ref), 0.0)
    ext_ref[_CHALO:_CHALO + tm, :] = glu(glu_ref)
    ext_ref[_CHALO + tm:n_ext, :] = jnp.where(has_next, glu(glun_ref), 0.0)
    e = ext_ref[...]
    for r in range(SUBLANE):
        sh_ref[r, SUBLANE - r:SUBLANE - r + n_ext, :] = e

    def row_block(rb):
        base = rb * _ROWBLK
        nsub = _ROWBLK // SUBLANE
        accs = [jnp.broadcast_to(cb_ref[...], (SUBLANE, CONF_CH))] * nsub
        for j in range(CONF_KERNEL):
            off = _CHALO - _CPAD + j
            qq, r = divmod(off, SUBLANE)
            w8 = cw_ref[j * SUBLANE:(j + 1) * SUBLANE, :]
            accs = [a + w8 * sh_ref[r, pl.ds(base + (qq + 1 + k) * SUBLANE, SUBLANE), :]
                    for k, a in enumerate(accs)]
        for k, a in enumerate(accs):
            conv_ref[pl.ds(base + k * SUBLANE, SUBLANE), :] = a

    for rb in range(tm // _ROWBLK):
        row_block(rb)

    u = conv_ref[...]
    mu = jnp.mean(u, axis=-1, keepdims=True)
    uc = u - mu
    var = jnp.mean(uc * uc, axis=-1, keepdims=True)
    u = _silu(uc * lax.rsqrt(var + EPS) * lng_ref[...] + lnb_ref[...])

    y = (yf_ref[...] + yb_ref[...]) * _silu(z_ref[...])
    y = _rms(y) * nrm_ref[...]

    out = _dot(y.astype(BF16), wo_ref[0:D_SSD, :]) + _dot(u.astype(BF16), wo_ref[D_SSD:, :])
    m = mod_ref[0]
    x1 = x_ref[...] + m[2:3] * out
    x1_ref[...] = x1
    hf_ref[...] = _modnorm(x1, gffn_ref[...], m[4:5], m[3:4]).astype(BF16)


def _mix_out(x2, yf, yb, z, glu, mod, nrm, cw, cb, lng, lnb, wo, gffn, seq, per_batch, tm):
    t = x2.shape[0]
    seq_tiles = seq // tm
    hb = tm // _CHALO
    last_hb = t // _CHALO - 1
    row = lambda w: pl.BlockSpec((tm, w), lambda i: (i, 0))
    vec = _const_spec((1, D_MODEL))
    return pl.pallas_call(
        functools.partial(_mixout_kernel, tm=tm, seq_tiles=seq_tiles),
        grid=(t // tm,),
        in_specs=[row(D_MODEL), row(D_SSD), row(D_SSD), row(D_SSD), row(2 * CONF_CH),
                  pl.BlockSpec((_CHALO, 2 * CONF_CH), lambda i: (jnp.maximum(i * hb - 1, 0), 0)),
                  pl.BlockSpec((_CHALO, 2 * CONF_CH), lambda i: (jnp.minimum((i + 1) * hb, last_hb), 0)),
                  _mod_spec(seq_tiles, per_batch),
                  vec, _const_spec(cw.shape), vec, vec, vec, _const_spec(wo.shape), vec],
        out_specs=[row(D_MODEL), row(D_MODEL)],
        out_shape=[jax.ShapeDtypeStruct((t, D_MODEL), F32), jax.ShapeDtypeStruct((t, D_MODEL), BF16)],
        scratch_shapes=[pltpu.VMEM((tm + 2 * _CHALO, CONF_CH), F32),
                        pltpu.VMEM((SUBLANE, tm + 2 * _CHALO + SUBLANE, CONF_CH), F32),
                        pltpu.VMEM((tm, CONF_CH), F32)],
        compiler_params=_cparams(("parallel",)),
        name="mix_out",
    )(x2, yf, yb, z, glu, glu, glu, mod, nrm, cw, cb, lng, lnb, wo, gffn)


_FF_CHUNK = 256
_FF_NCH = D_FF // _FF_CHUNK
_FHALO = 128
_FF_ROWBLK = 32
_FF_PAD = SUBLANE
_FFN_FLAGS = None


def _ffn_kernel(*refs, tm, seq_tiles, period, row_taps, halo, final_norm):
    refs = list(refs)
    hf_ref = refs.pop(0)
    if halo:
        hfp_ref, hfn_ref = refs.pop(0), refs.pop(0)
    x1_ref, mod_ref, wup_ref, cw_ref, wdn_ref = [refs.pop(0) for _ in range(5)]
    if final_norm:
        nf_ref, o_ref = refs.pop(0), refs.pop(0)
    else:
        modn_ref, gn_ref, o_ref, hn_ref = [refs.pop(0) for _ in range(4)]
    ext_ref = refs.pop(0) if halo else hf_ref
    sl_ref, sc_ref, sr_ref, ml_ref, mr_ref, g_ref, acc_ref = refs
    i = pl.program_id(0)
    pad = _FF_PAD
    n_ext = tm + 2 * halo

    if halo:
        has_prev = i % seq_tiles > 0
        has_next = i % seq_tiles < seq_tiles - 1
        ext_ref[0:halo, :] = jnp.where(has_prev, hfp_ref[_FHALO - halo:_FHALO, :], 0.0).astype(BF16)
        ext_ref[halo:halo + tm, :] = hf_ref[...]
        ext_ref[halo + tm:n_ext, :] = jnp.where(has_next, hfn_ref[0:halo, :], 0.0).astype(BF16)

    col = (lax.broadcasted_iota(jnp.int32, (n_ext, _FF_CHUNK), 0) + (period - halo % period)) % period
    ml_ref[...] = jnp.where(col == period - 1, 0.0, 1.0)
    mr_ref[...] = jnp.where(col == 0, 0.0, 1.0)
    zeros_pad = jnp.zeros((2 * pad, _FF_CHUNK), F32)
    for ref in (sl_ref, sr_ref):
        for k in range(4):
            ref[k, 0:2 * pad, :] = zeros_pad
            ref[k, n_ext:n_ext + 2 * pad, :] = zeros_pad
    acc_ref[...] = jnp.zeros((tm, D_MODEL), F32)

    def up(c, slot):
        for half in range(2):
            c0 = half * D_FF + c * _FF_CHUNK
            u = _dot(ext_ref[...], wup_ref[:, c0:c0 + _FF_CHUNK])
            k = slot * 2 + half
            sc_ref[k, pad:pad + n_ext, :] = u
            sl_ref[k, pad + 1:pad + 1 + n_ext, :] = u * ml_ref[...]
            sr_ref[k, pad - 1:pad - 1 + n_ext, :] = u * mr_ref[...]

    def conv(c, slot):
        w = [cw_ref[:, half * D_FF + c * _FF_CHUNK:half * D_FF + (c + 1) * _FF_CHUNK] for half in range(2)]
        for rb in range(tm // _FF_ROWBLK):
            r0 = rb * _FF_ROWBLK
            outs = []
            for half in range(2):
                k = slot * 2 + half
                acc = None
                for dr in row_taps:
                    base = pad + halo + dr * period + r0
                    for dc, ref in enumerate((sl_ref, sc_ref, sr_ref)):
                        tap = (dr + 1) * 3 + dc
                        term = w[half][tap:tap + 1, :] * ref[k, pl.ds(base, _FF_ROWBLK), :]
                        acc = term if acc is None else acc + term
                outs.append(acc)
            g_ref[slot, r0:r0 + _FF_ROWBLK, :] = (outs[0] * _silu(outs[1])).astype(BF16)

    def down(c, slot):
        acc_ref[...] += _dot(g_ref[slot], wdn_ref[c * _FF_CHUNK:(c + 1) * _FF_CHUNK, :])

    last = _FF_NCH - 1
    up(0, 0)
    up(1, 1)
    conv(0, 0)

    for c in range(1, last + 1):
        if c < last:
            up(c + 1, (c + 1) % 2)
        conv(c, c % 2)
        down(c - 1, (c - 1) % 2)
    down(last, last % 2)

    m = mod_ref[0]
    x2 = x1_ref[...] + m[5:6] * acc_ref[...]
    if final_norm:
        o_ref[...] = _rms(x2) * nf_ref[...]
    else:
        o_ref[...] = x2
        mn = modn_ref[0]
        hn_ref[...] = _modnorm(x2, gn_ref[...], mn[1:2], mn[0:1])


def _conv_ffn(hf, x1, mod, wup, cw, wdn, tail, seq, per_batch, tm, latent, final_norm):
    t = hf.shape[0]
    seq_tiles = seq // tm
    if latent:
        period, row_taps, halo = GRID_W, (-1, 0, 1), GRID_W + 16
    else:
        period, row_taps, halo = seq, (0,), 0
    n_ext = tm + 2 * halo
    row = lambda w: pl.BlockSpec((tm, w), lambda i: (i, 0))
    in_specs = [row(D_MODEL)]
    args = [hf]
    scratch = []
    if halo:
        hb = tm // _FHALO
        last_hb = t // _FHALO - 1
        in_specs += [pl.BlockSpec((_FHALO, D_MODEL), lambda i: (jnp.maximum(i * hb - 1, 0), 0)),
                     pl.BlockSpec((_FHALO, D_MODEL), lambda i: (jnp.minimum((i + 1) * hb, last_hb), 0))]
        args += [hf, hf]
        scratch.append(pltpu.VMEM((n_ext, D_MODEL), BF16))
    in_specs += [row(D_MODEL), _mod_spec(seq_tiles, per_batch), _const_spec(wup.shape, single=True),
                 _const_spec(cw.shape), _const_spec(wdn.shape, single=True)]
    args += [x1, mod, wup, cw, wdn]
    vec = _const_spec((1, D_MODEL))
    if final_norm:
        in_specs += [vec]
        out_specs = row(D_MODEL)
        out_shape = jax.ShapeDtypeStruct((t, D_MODEL), F32)
    else:
        in_specs += [_mod_spec(seq_tiles, per_batch), vec]
        out_specs = [row(D_MODEL), row(D_MODEL)]
        out_shape = [jax.ShapeDtypeStruct((t, D_MODEL), F32)] * 2
    args += list(tail)
    ubuf = pltpu.VMEM((4, n_ext + 2 * _FF_PAD, _FF_CHUNK), F32)
    mask = pltpu.VMEM((n_ext, _FF_CHUNK), F32)
    scratch += [ubuf, ubuf, ubuf, mask, mask,
                pltpu.VMEM((2, tm, _FF_CHUNK), BF16), pltpu.VMEM((tm, D_MODEL), F32)]
    return pl.pallas_call(
        functools.partial(_ffn_kernel, tm=tm, seq_tiles=seq_tiles, period=period,
                          row_taps=row_taps, halo=halo, final_norm=final_norm),
        grid=(t // tm,),
        in_specs=in_specs,
        out_specs=out_specs,
        out_shape=out_shape,
        scratch_shapes=scratch,
        compiler_params=_cparams(("parallel",), _FFN_FLAGS),
        name="conv_ffn",
    )(*args)


def _dft_tables(n):
    k = np.arange(n, dtype=np.int64)
    ang = 2.0 * np.pi * ((k[:, None] * k[None, :]) % n) / n
    s = 1.0 / math.sqrt(n)
    return np.cos(ang) * s, np.sin(ang) * s


def _seqdft_direct_kernel(h_ref, m_ref, pr_ref, pi_ref, *, seq):
    p = _dot(m_ref[...], h_ref[...].astype(BF16))
    pr_ref[...] = p[0:seq].astype(BF16)
    pi_ref[...] = p[seq:2 * seq].astype(BF16)


def _seq_dft_direct(h2, nb, seq):
    c, s = _dft_tables(seq)
    mat = jnp.asarray(np.concatenate([c, -s], axis=0), BF16)
    t = h2.shape[0]
    tc = 512
    blk = pl.BlockSpec((seq, tc), lambda b, j: (b, j))
    return pl.pallas_call(
        functools.partial(_seqdft_direct_kernel, seq=seq),
        grid=(nb, D_MODEL // tc),
        in_specs=[blk, _const_spec(mat.shape)],
        out_specs=[blk, blk],
        out_shape=[jax.ShapeDtypeStruct((t, D_MODEL), BF16)] * 2,
        compiler_params=_cparams(("parallel", "parallel")), name="seq_dft_direct",
    )(h2, mat)


_DFT_PITCH = GRID_W + 4
_DFT_TC = 2 * LANE


def _seqdft_2stage_kernel(h_ref, m1_ref, m2_ref, pr_ref, pi_ref, xa_ref, yr_ref, yi_ref, pb_ref, *, w):
    nlb = _DFT_TC // LANE
    pitch = _DFT_PITCH
    lanes = lambda v, j: v[:, j * LANE:(j + 1) * LANE]

    def gather(ref, r):
        return jnp.concatenate([ref[j, pl.ds(r, w, stride=pitch), :] for j in range(nlb)], axis=1)

    for n1 in range(w):
        rows = h_ref[n1 * w:(n1 + 1) * w, :]
        for j in range(nlb):
            xa_ref[j, n1 * pitch:n1 * pitch + w, :] = lanes(rows, j)
    for n2 in range(w):
        y = _dot(m1_ref[n2], gather(xa_ref, n2).astype(BF16))
        for j in range(nlb):
            yr_ref[j, n2 * pitch:n2 * pitch + w, :] = lanes(y[0:w], j)
            yi_ref[j, n2 * pitch:n2 * pitch + w, :] = lanes(y[w:2 * w], j)
    m2 = m2_ref[...]
    for k1 in range(w):
        yk = jnp.concatenate([gather(yr_ref, k1), gather(yi_ref, k1)], axis=0).astype(BF16)
        p = _dot(m2, yk)
        for j in range(nlb):
            xa_ref[j, pl.ds(k1, w, stride=pitch), :] = lanes(p[0:w], j)
            pb_ref[j, pl.ds(k1, w, stride=pitch), :] = lanes(p[w:2 * w], j)
    for k2 in range(w):
        for ref, out in ((xa_ref, pr_ref), (pb_ref, pi_ref)):
            out[k2 * w:(k2 + 1) * w, :] = jnp.concatenate(
                [ref[j, k2 * pitch:k2 * pitch + w, :] for j in range(nlb)], axis=1).astype(BF16)


def _seq_dft_2stage(h2, nb, seq):
    w = GRID_W
    assert seq == w * w
    k1 = np.arange(w, dtype=np.int64)[:, None]
    n1 = np.arange(w, dtype=np.int64)[None, :]
    m1 = np.zeros((w, 2 * w, w), np.float64)
    for n2 in range(w):
        ang = 2.0 * np.pi * ((k1 * (w * n1 + n2)) % seq) / seq
        m1[n2, :w] = np.cos(ang)
        m1[n2, w:] = -np.sin(ang)
    m1 /= math.sqrt(seq)
    ang2 = 2.0 * np.pi * ((k1 * n1) % w) / w
    c2, s2 = np.cos(ang2), np.sin(ang2)
    m2 = np.block([[c2, s2], [-s2, c2]])
    t = h2.shape[0]
    tc = _DFT_TC
    blk = pl.BlockSpec((seq, tc), lambda b, j: (b, j))
    pitched = pltpu.VMEM((tc // LANE, w * _DFT_PITCH, LANE), F32)
    return pl.pallas_call(
        functools.partial(_seqdft_2stage_kernel, w=w),
        grid=(nb, D_MODEL // tc),
        in_specs=[blk, _const_spec(m1.shape), _const_spec(m2.shape)],
        out_specs=[blk, blk],
        out_shape=[jax.ShapeDtypeStruct((t, D_MODEL), BF16)] * 2,
        scratch_shapes=[pitched] * 4,
        compiler_params=_cparams(("parallel", "parallel")), name="seq_dft_2stage",
    )(h2, jnp.asarray(m1, BF16), jnp.asarray(m2, BF16))


def _fnetw_kernel(dft_ref, w_ref, o_ref):
    o_ref[...] = _dot(dft_ref[...], w_ref[...].astype(BF16)).astype(BF16)


def _fnet_weights(fnet_w):
    gsz = D_MODEL // FNET_GROUPS
    c, s = _dft_tables(gsz)
    eye = np.eye(FNET_GROUPS)
    dft = jnp.asarray(np.concatenate([np.kron(eye, c), np.kron(eye, s)], axis=0), BF16)
    tmw = 512
    return pl.pallas_call(
        _fnetw_kernel, grid=(2 * D_MODEL // tmw,),
        in_specs=[pl.BlockSpec((tmw, D_MODEL), lambda i: (i, 0)), _const_spec((D_MODEL, D_MODEL))],
        out_specs=pl.BlockSpec((tmw, D_MODEL), lambda i: (i, 0)),
        out_shape=jax.ShapeDtypeStruct((2 * D_MODEL, D_MODEL), BF16),
        compiler_params=_cparams(("parallel",)), name="fnet_weights",
    )(dft, fnet_w)


def _fnetout_kernel(pr_ref, pi_ref, x_ref, mod_ref, w_ref, b_ref, gffn_ref, x1_ref, hf_ref):
    out = _dot(pr_ref[...], w_ref[0:D_MODEL, :]) + _dot(pi_ref[...], w_ref[D_MODEL:, :]) + b_ref[...]
    m = mod_ref[0]
    x1 = x_ref[...] + m[2:3] * out
    x1_ref[...] = x1
    hf_ref[...] = _modnorm(x1, gffn_ref[...], m[4:5], m[3:4]).astype(BF16)


def _fnet_out(pr, pi, x2, mod, wcs, b, gffn, seq, per_batch, tm):
    t = x2.shape[0]
    row = pl.BlockSpec((tm, D_MODEL), lambda i: (i, 0))
    vec = _const_spec((1, D_MODEL))
    return pl.pallas_call(
        _fnetout_kernel, grid=(t // tm,),
        in_specs=[row, row, row, _mod_spec(seq // tm, per_batch), _const_spec(wcs.shape), vec, vec],
        out_specs=[row, row],
        out_shape=[jax.ShapeDtypeStruct((t, D_MODEL), F32), jax.ShapeDtypeStruct((t, D_MODEL), BF16)],
        compiler_params=_cparams(("parallel",)), name="fnet_out",
    )(pr, pi, x2, mod, wcs, b, gffn)


def _trunk(x, mods, h0t, per_batch, latent, p):
    nb, seq, _ = x.shape
    t = nb * seq
    x2 = x.reshape(t, D_MODEL)
    tm = 256

    z, xbc, glu, dt = _in_proj(x2, mods[0], p['g_mix'][0], p['wz'], p['wx'], p['wg'], p['wd'],
                               p['ssd_cw'], p['ssd_cb'], seq, per_batch, tm)
    scan = _ssd_scan(xbc, dt, h0t, p['dtbias'], p['alog'], p['dskip'], nb, seq, emit_state=not latent)
    yf, yb = scan[0], scan[1]
    state = scan[2] if not latent else None
    x1, hf = _mix_out(x2, yf, yb, z, glu, mods[0], p['ssd_norm'], p['conf_cw'], p['conf_cb'],
                      p['ln_g'], p['ln_b'], p['w_out'], p['g_ffn'][0], seq, per_batch, tm)
    ftm = 512
    x2, hm = _conv_ffn(hf, x1, mods[0], p['w_up'][0], p['ffn_cw'][0], p['w_down'][0],
                       (mods[1], p['g_mix'][1]), seq, per_batch, ftm, latent, final_norm=False)

    if seq == GRID_W * GRID_W:
        pr, pi = _seq_dft_2stage(hm, nb, seq)
    else:
        pr, pi = _seq_dft_direct(hm, nb, seq)
    x1, hf = _fnet_out(pr, pi, x2, mods[1], p['fnet_wcs'], p['fnet_b'], p['g_ffn'][1],
                       seq, per_batch, tm if seq == tm else ftm)
    y = _conv_ffn(hf, x1, mods[1], p['w_up'][1], p['ffn_cw'][1], p['w_down'][1], (p['norm_f'],),
                  seq, per_batch, ftm, latent, final_norm=True)
    return y.reshape(nb, seq, D_MODEL), state


def kernel(x_prompt, x_sample, c, state_ssd, c_ctx, w_mod, b_mod, g_mix, g_ffn, w_in, ssd_conv_w, ssd_conv_b, ssd_dt_bias, ssd_a_log, ssd_d, ssd_norm, conf_conv_w, conf_conv_b, conf_ln_g, conf_ln_b, w_out, fnet_w, fnet_b, ffn_w_up, ffn_conv_w, ffn_w_down, norm_f):
    depth = w_mod.shape[0]
    nb_ctx = x_prompt.shape[0]
    nb_lat = x_sample.shape[0]
    assert nb_lat <= CTX_ROW

    cond = jnp.zeros((MOD_ROWS, D_MODEL), F32).at[:nb_lat].set(c).at[CTX_ROW].set(c_ctx)
    mod = _modulation(cond, w_mod, b_mod).reshape(depth, MOD_ROWS, N_MOD, D_MODEL)
    mods = [mod[i] for i in range(depth)]

    def pad_lanes(v, fill=0.0):
        return jnp.pad(v, ((0, 0), (0, LANE - v.shape[1])), constant_values=fill)

    w_in0 = w_in[0].astype(BF16)
    o_dt = D_SSD + D_XBC
    o_glu = o_dt + 2 * SSD_HEADS
    vec = lambda v: v.reshape(1, -1)
    p = dict(
        g_mix=[vec(g_mix[i]) for i in range(depth)],
        g_ffn=[vec(g_ffn[i]) for i in range(depth)],
        wz=w_in0[:, :D_SSD], wx=w_in0[:, D_SSD:o_dt], wd=pad_lanes(w_in0[:, o_dt:o_glu]), wg=w_in0[:, o_glu:],
        ssd_cw=jnp.repeat(ssd_conv_w[0], SUBLANE, axis=0),
        ssd_cb=vec(ssd_conv_b[0]),
        dtbias=pad_lanes(ssd_dt_bias[0].reshape(1, -1)),
        alog=pad_lanes(ssd_a_log[0].reshape(1, -1)),
        dskip=vec(jnp.repeat(ssd_d[0], SSD_HEAD_DIM)),
        ssd_norm=vec(ssd_norm[0]),
        conf_cw=jnp.repeat(conf_conv_w[0], SUBLANE, axis=0),
        conf_cb=vec(conf_conv_b[0]), ln_g=vec(conf_ln_g[0]), ln_b=vec(conf_ln_b[0]),
        w_out=w_out[0].astype(BF16),
        fnet_wcs=_fnet_weights(fnet_w[0]), fnet_b=vec(fnet_b[0]),
        w_up=[ffn_w_up[i].astype(BF16) for i in range(depth)],
        ffn_cw=[jnp.pad(ffn_conv_w[i].reshape(9, -1), ((0, 7), (0, 0))) for i in range(depth)],
        w_down=[ffn_w_down[i].astype(BF16) for i in range(depth)],
        norm_f=vec(norm_f),
    )

    h0_ctx = jnp.zeros((nb_ctx, 2, D_SSD, SSD_STATE), F32)
    y_prompt, st = _trunk(x_prompt, mods, h0_ctx, False, False, p)
    new_state = st.reshape(nb_ctx, 1, 2, SSD_HEADS, SSD_HEAD_DIM, SSD_STATE).astype(x_prompt.dtype)

    h0_lat = state_ssd[:, 0].astype(F32).reshape(nb_lat, 2, D_SSD, SSD_STATE)
    y_sample, _ = _trunk(x_sample, mods, h0_lat, True, True, p)
    return (y_prompt, y_sample, new_state)
```
